```python
import math
import jax, jax.numpy as jnp
from jax import lax
import numpy as np

D_MODEL = 2048
BATCH = 1
SEQ = 8192
DEPTH = 4
DEC_BATCH = 8
DEC_SEQ = 2048
PAST_LEN = 128

D_INNER = 2 * D_MODEL
SSD_HEAD_DIM = 64
SSD_HEADS = D_INNER // SSD_HEAD_DIM
SSD_GROUPS = 8
HEADS_PER_GROUP = SSD_HEADS // SSD_GROUPS
D_STATE = 128
D_CONV = 5
CONV_PAD = D_CONV // 2
CONV_DIM = D_INNER + 2 * SSD_GROUPS * D_STATE
CHUNK = 128
D_GMLP = D_MODEL
GMLP_GROUPS = 8
GMLP_GROUP_DIM = D_GMLP // GMLP_GROUPS
OFF_XBC = D_INNER
OFF_DT = OFF_XBC + CONV_DIM
OFF_UV = OFF_DT + 2 * SSD_HEADS
OFF_GATE = OFF_UV + 2 * D_GMLP
N_IN = OFF_GATE + 2 * D_MODEL
N_EXPERT_GROUPS = 8
EXPERTS_PER_GROUP = 8
N_EXPERTS = N_EXPERT_GROUPS * EXPERTS_PER_GROUP
TOP_K = 2
D_EXPERT = D_MODEL // 4
ROUTE_BLOCK = 128
ALPHA = (2.0 * DEPTH) ** 0.25
BETA = (8.0 * DEPTH) ** -0.25
EPS = 1e-5

kernel_name = "hybrid_ssd_gmlp_hmoe_encoder"


def _layer_norm(x, g, b):
    xf = x.astype(jnp.float32)
    mu = jnp.mean(xf, axis=-1, keepdims=True)
    xc = xf - mu
    var = jnp.mean(xc * xc, axis=-1, keepdims=True)
    return (xc * lax.rsqrt(var + EPS) * g.astype(jnp.float32) + b.astype(jnp.float32)).astype(x.dtype)


def _segsum_exp(a_cum):
    diff = a_cum[..., :, None] - a_cum[..., None, :]
    mask = jnp.tril(jnp.ones((CHUNK, CHUNK), dtype=bool))
    return jnp.exp(jnp.where(mask, diff, -jnp.inf))


def _ssd_chunked(xdt, a, bm, cm):
    b, L = xdt.shape[0], xdt.shape[1]
    nc = L // CHUNK
    dt_ = xdt.dtype
    x = xdt.reshape(b, nc, CHUNK, SSD_GROUPS, HEADS_PER_GROUP, SSD_HEAD_DIM)
    bm = bm.reshape(b, nc, CHUNK, SSD_GROUPS, D_STATE)
    cm = cm.reshape(b, nc, CHUNK, SSD_GROUPS, D_STATE)
    a = a.reshape(b, nc, CHUNK, SSD_GROUPS, HEADS_PER_GROUP).transpose(0, 1, 3, 4, 2)
    a_cum = jnp.cumsum(a, axis=-1)
    cb = jnp.einsum('bclgn,bcsgn->bcgls', cm, bm)
    scores = cb[:, :, :, None] * _segsum_exp(a_cum).astype(dt_)
    y_diag = jnp.einsum('bcgels,bcsgep->bclgep', scores, x)
    decay_to_end = jnp.exp(a_cum[..., -1:] - a_cum).astype(dt_).transpose(0, 1, 4, 2, 3)
    chunk_states = jnp.einsum('bcsgn,bcsgep->bcgepn', bm, x * decay_to_end[..., None])
    chunk_decay = jnp.exp(a_cum[..., -1]).astype(dt_)

    def step(h, inp):
        s_c, d_c = inp
        return h * d_c[..., None, None] + s_c, h

    h0 = jnp.zeros((b, SSD_GROUPS, HEADS_PER_GROUP, SSD_HEAD_DIM, D_STATE), dt_)
    _, prev = lax.scan(step, h0, (jnp.moveaxis(chunk_states, 1, 0), jnp.moveaxis(chunk_decay, 1, 0)))
    prev = jnp.moveaxis(prev, 0, 1)
    decay_from_start = jnp.exp(a_cum).astype(dt_).transpose(0, 1, 4, 2, 3)
    y_off = jnp.einsum('bclgn,bcgepn->bclgep', cm, prev) * decay_from_start[..., None]
    return (y_diag + y_off).reshape(b, L, SSD_HEADS, SSD_HEAD_DIM)


def _mixer(x, w_in, conv_w, conv_b, dt_bias, a_log, d_skip, ssd_norm_w,
           sgu_norm_w, sgu_norm_b, sgu_w, sgu_b, w_branch, w_out):
    b, L, _ = x.shape
    nc = L // CHUNK
    proj = jnp.einsum('bld,dn->bln', x, w_in)
    z, xbc, dt_raw, uv, gates = jnp.split(proj, [OFF_XBC, OFF_DT, OFF_UV, OFF_GATE], axis=-1)

    xbc = lax.conv_general_dilated(xbc, conv_w[:, None, :], window_strides=(1,),
                                   padding=[(CONV_PAD, CONV_PAD)],
                                   dimension_numbers=('NWC', 'WIO', 'NWC'),
                                   feature_group_count=CONV_DIM)
    xbc = jax.nn.silu(xbc + conv_b)
    xs, bm, cm = jnp.split(xbc, [D_INNER, D_INNER + SSD_GROUPS * D_STATE], axis=-1)
    xs = xs.reshape(b, L, SSD_HEADS, SSD_HEAD_DIM)
    bm = bm.reshape(b, L, SSD_GROUPS, D_STATE)
    cm = cm.reshape(b, L, SSD_GROUPS, D_STATE)
    dt = jax.nn.softplus(dt_raw.astype(jnp.float32) + dt_bias.reshape(2 * SSD_HEADS).astype(jnp.float32))
    dt = dt.reshape(b, L, 2, SSD_HEADS)
    A = -jnp.exp(a_log.astype(jnp.float32))
    dt_f, dt_b = dt[:, :, 0], dt[:, :, 1]
    y_f = _ssd_chunked(xs * dt_f[..., None].astype(xs.dtype), A[0] * dt_f, bm, cm)
    y_b = jnp.flip(_ssd_chunked(jnp.flip(xs * dt_b[..., None].astype(xs.dtype), 1),
                                jnp.flip(A[1] * dt_b, 1), jnp.flip(bm, 1), jnp.flip(cm, 1)), 1)
    y = y_f + y_b + xs * d_skip[:, None]
    y = y.reshape(b, L, D_INNER) * jax.nn.silu(z)
    yg = y.reshape(b, L, SSD_GROUPS, D_INNER // SSD_GROUPS).astype(jnp.float32)
    yg = yg * lax.rsqrt(jnp.mean(yg * yg, axis=-1, keepdims=True) + EPS)
    y_ssd = (yg.reshape(b, L, D_INNER) * ssd_norm_w.astype(jnp.float32)).astype(x.dtype)

    u, v = jnp.split(jax.nn.gelu(uv), 2, axis=-1)
    v = _layer_norm(v, sgu_norm_w, sgu_norm_b)
    v = v.reshape(b, nc, CHUNK, GMLP_GROUPS, GMLP_GROUP_DIM)
    v_mix = jnp.einsum('gts,bcsgd->bctgd', sgu_w, v) + sgu_b.T[:, :, None]
    y_gmlp = u * v_mix.reshape(b, L, D_GMLP)

    p_a = jnp.einsum('blk,kd->bld', y_ssd, w_branch[:D_INNER])
    p_b = jnp.einsum('blk,kd->bld', y_gmlp, w_branch[D_INNER:])
    g_a, g_b = jnp.split(jax.nn.sigmoid(gates), 2, axis=-1)
    merged = g_a * p_a + g_b * p_b
    return jnp.einsum('bld,de->ble', merged, w_out)


def _hier_moe(x, w_rg, b_rg, w_re, b_re, w_gate, w_up, w_down):
    b, L, d = x.shape
    T = b * L
    xf = x.reshape(T, d)
    g_prob = jax.nn.softmax((xf @ w_rg + b_rg).astype(jnp.float32), axis=-1)
    g_w, g_idx = lax.top_k(g_prob, 1)
    e_logits = (xf @ w_re + b_re).astype(jnp.float32).reshape(T, N_EXPERT_GROUPS, EXPERTS_PER_GROUP)
    e_sel = jnp.take_along_axis(e_logits, g_idx[:, :, None], axis=1)[:, 0]
    e_w, e_idx = lax.top_k(jax.nn.softmax(e_sel, axis=-1), TOP_K)
    e_w = e_w / jnp.sum(e_w, axis=-1, keepdims=True)
    weights = g_w * e_w
    expert = g_idx * EXPERTS_PER_GROUP + e_idx

    A_n = T * TOP_K
    e_flat = expert.reshape(A_n)
    w_flat = weights.reshape(A_n)
    tok_flat = jnp.repeat(jnp.arange(T, dtype=jnp.int32), TOP_K)
    order = jnp.argsort(e_flat)
    e_sorted = e_flat[order]
    counts = jnp.bincount(e_flat, length=N_EXPERTS)
    padded = (counts + ROUTE_BLOCK - 1) // ROUTE_BLOCK * ROUTE_BLOCK
    starts = jnp.cumsum(counts) - counts
    pends = jnp.cumsum(padded)
    pstarts = pends - padded
    dest = pstarts[e_sorted] + (jnp.arange(A_n) - starts[e_sorted])
    n_blocks = -(-A_n // ROUTE_BLOCK) + N_EXPERTS
    P = n_blocks * ROUTE_BLOCK
    slot_tok = jnp.full((P,), T, jnp.int32).at[dest].set(tok_flat[order])
    slot_w = jnp.zeros((P,), x.dtype).at[dest].set(w_flat[order].astype(x.dtype))
    block_expert = jnp.clip(jnp.searchsorted(pends, jnp.arange(n_blocks) * ROUTE_BLOCK, side='right'),
                            0, N_EXPERTS - 1)
    x_pad = jnp.concatenate([xf, jnp.zeros((1, d), xf.dtype)], axis=0)
    xb = x_pad[slot_tok].reshape(n_blocks, ROUTE_BLOCK, d)

    def expert_block(args):
        xblk, e = args
        h = jax.nn.silu(xblk @ w_gate[e]) * (xblk @ w_up[e])
        return h @ w_down[e]

    yb = lax.map(expert_block, (xb, block_expert)).reshape(P, d)
    y = jax.ops.segment_sum(yb * slot_w[:, None], slot_tok, num_segments=T + 1)[:T]
    return y.reshape(b, L, d)


def _trunk(x, w_in, conv_w, conv_b, dt_bias, a_log, d_skip, ssd_norm_w, sgu_norm_w, sgu_norm_b,
           sgu_w, sgu_b, w_branch, w_out, ln1_g, ln1_b, ln2_g, ln2_b,
           w_router_group, b_router_group, w_router_expert, b_router_expert, w_gate, w_up, w_down):
    for i in range(DEPTH):
        mix = _mixer(x, w_in[i], conv_w[i], conv_b[i], dt_bias[i], a_log[i], d_skip[i], ssd_norm_w[i],
                     sgu_norm_w[i], sgu_norm_b[i], sgu_w[i], sgu_b[i], w_branch[i], w_out[i])
        x = _layer_norm(ALPHA * x + mix, ln1_g[i], ln1_b[i])
        ffn = _hier_moe(x, w_router_group[i], b_router_group[i], w_router_expert[i], b_router_expert[i],
                        w_gate[i], w_up[i], w_down[i])
        x = _layer_norm(ALPHA * x + ffn, ln2_g[i], ln2_b[i])
    return x


def setup_inputs(seed: int = 0) -> dict:
    key = jax.random.key(seed)
    ks = jax.random.split(key, 27)
    f32 = jnp.float32

    def nrm(k, shape, scale):
        return jax.random.normal(k, shape, f32) * scale

    x_prompt = nrm(ks[0], (BATCH, SEQ, D_MODEL), 1.0)
    x_sample = nrm(ks[1], (DEC_BATCH, DEC_SEQ, D_MODEL), 1.0)
    w_in = nrm(ks[2], (DEPTH, D_MODEL, N_IN), D_MODEL ** -0.5)
    conv_w = nrm(ks[3], (DEPTH, D_CONV, CONV_DIM), D_CONV ** -0.5)
    conv_b = nrm(ks[4], (DEPTH, CONV_DIM), 0.02)
    dt0 = jnp.exp(jax.random.uniform(ks[5], (DEPTH, 2, SSD_HEADS), f32, math.log(1e-3), math.log(1e-1)))
    dt_bias = dt0 + jnp.log(-jnp.expm1(-dt0))
    a_log = jnp.log(jax.random.uniform(ks[6], (DEPTH, 2, SSD_HEADS), f32, 1.0, 16.0))
    d_skip = 1.0 + nrm(ks[7], (DEPTH, SSD_HEADS), 0.1)
    ssd_norm_w = 1.0 + nrm(ks[8], (DEPTH, D_INNER), 0.1)
    sgu_norm_w = 1.0 + nrm(ks[9], (DEPTH, D_GMLP), 0.1)
    sgu_norm_b = nrm(ks[10], (DEPTH, D_GMLP), 0.02)
    sgu_w = nrm(ks[11], (DEPTH, GMLP_GROUPS, CHUNK, CHUNK), CHUNK ** -0.5)
    sgu_b = 1.0 + nrm(ks[12], (DEPTH, GMLP_GROUPS, CHUNK), 0.1)
    w_branch = jnp.concatenate([nrm(ks[13], (DEPTH, D_INNER, D_MODEL), D_INNER ** -0.5),
                                nrm(ks[14], (DEPTH, D_GMLP, D_MODEL), D_GMLP ** -0.5)], axis=1)
    w_out = nrm(ks[15], (DEPTH, D_MODEL, D_MODEL), D_MODEL ** -0.5 * BETA)
    ln1_g = 1.0 + nrm(ks[16], (DEPTH, D_MODEL), 0.1)
    ln1_b = nrm(ks[17], (DEPTH, D_MODEL), 0.02)
    ln2_g = 1.0 + nrm(ks[18], (DEPTH, D_MODEL), 0.1)
    ln2_b = nrm(ks[19], (DEPTH, D_MODEL), 0.02)
    w_router_group = nrm(ks[20], (DEPTH, D_MODEL, N_EXPERT_GROUPS), D_MODEL ** -0.5)
    b_router_group = nrm(ks[21], (DEPTH, N_EXPERT_GROUPS), 0.01)
    w_router_expert = nrm(ks[22], (DEPTH, D_MODEL, N_EXPERTS), D_MODEL ** -0.5)
    b_router_expert = nrm(ks[23], (DEPTH, N_EXPERTS), 0.01)
    w_gate = nrm(ks[24], (DEPTH, N_EXPERTS, D_MODEL, D_EXPERT), D_MODEL ** -0.5)
    w_up = nrm(ks[25], (DEPTH, N_EXPERTS, D_MODEL, D_EXPERT), D_MODEL ** -0.5)
    w_down = nrm(ks[26], (DEPTH, N_EXPERTS, D_EXPERT, D_MODEL), D_EXPERT ** -0.5 * BETA)
    return {"x_prompt": x_prompt, "x_sample": x_sample, "w_in": w_in, "conv_w": conv_w, "conv_b": conv_b,
            "dt_bias": dt_bias, "a_log": a_log, "d_skip": d_skip, "ssd_norm_w": ssd_norm_w,
            "sgu_norm_w": sgu_norm_w, "sgu_norm_b": sgu_norm_b, "sgu_w": sgu_w, "sgu_b": sgu_b,
            "w_branch": w_branch, "w_out": w_out, "ln1_g": ln1_g, "ln1_b": ln1_b, "ln2_g": ln2_g, "ln2_b": ln2_b,
            "w_router_group": w_router_group, "b_router_group": b_router_group,
            "w_router_expert": w_router_expert, "b_router_expert": b_router_expert,
            "w_gate": w_gate, "w_up": w_up, "w_down": w_down}


def reference(x_prompt, x_sample, w_in, conv_w, conv_b, dt_bias, a_log, d_skip, ssd_norm_w,
              sgu_norm_w, sgu_norm_b, sgu_w, sgu_b, w_branch, w_out, ln1_g, ln1_b, ln2_g, ln2_b,
              w_router_group, b_router_group, w_router_expert, b_router_expert, w_gate, w_up, w_down):
    params = (w_in, conv_w, conv_b, dt_bias, a_log, d_skip, ssd_norm_w, sgu_norm_w, sgu_norm_b,
              sgu_w, sgu_b, w_branch, w_out, ln1_g, ln1_b, ln2_g, ln2_b,
              w_router_group, b_router_group, w_router_expert, b_router_expert, w_gate, w_up, w_down)
    y_prompt = _trunk(x_prompt, *params)
    y_sample = _trunk(x_sample, *params)
    return (y_prompt, y_sample)
```

```python
import functools
import math

import numpy as np
import jax
import jax.numpy as jnp
from jax import lax
from jax.experimental import pallas as pl
from jax.experimental.pallas import tpu as pltpu

F32 = jnp.float32
BF16 = jnp.bfloat16

D_MODEL = 2048
DEPTH = 4
D_INNER = 2 * D_MODEL
HEAD_DIM = 64
N_HEADS = D_INNER // HEAD_DIM
N_GROUPS = 8
GROUP_W = D_INNER // N_GROUPS
D_STATE = 128
D_CONV = 5
CONV_PAD = D_CONV // 2
BC_W = N_GROUPS * D_STATE
CONV_DIM = D_INNER + 2 * BC_W
CHUNK = 128
D_GMLP = D_MODEL
GMLP_GROUPS = 8
GMLP_GW = D_GMLP // GMLP_GROUPS
OFF_XBC = D_INNER
OFF_DT = OFF_XBC + CONV_DIM
OFF_UV = OFF_DT + 2 * N_HEADS
OFF_GATE = OFF_UV + 2 * D_GMLP
N_IN = OFF_GATE + 2 * D_MODEL
N_EGROUPS = 8
E_PER_GROUP = 8
N_EXPERTS = N_EGROUPS * E_PER_GROUP
TOP_K = 2
D_EXPERT = D_MODEL // 4
ALPHA = (2.0 * DEPTH) ** 0.25
EPS = 1e-5

LANE = 128
SUBLANE = 8
V7X_VMEM_BYTES = 64 * 1024 * 1024
VMEM_LIMIT = V7X_VMEM_BYTES * 3 // 4

MM_TM = 1024
MM_TN = 1024
CONV_TM = 512
CONV_TC = 512
MERGE_TM = 512
MERGE_TN = 512
LN_TM = 256
ROUTE_TM = 512
EXPERT_BLK = 256
ROUTE_W = LANE


def _cparams(*sem):
    return pltpu.CompilerParams(dimension_semantics=sem, vmem_limit_bytes=VMEM_LIMIT)


def _silu(x):
    return x * jax.nn.sigmoid(x)


def _softplus(x):
    return jnp.maximum(x, 0.0) + jnp.log1p(jnp.exp(-jnp.abs(x)))


def _gelu_tanh(x):
    c = math.sqrt(2.0 / math.pi)
    return x * (0.5 * (1.0 + jnp.tanh(c * (x + 0.044715 * (x * x * x)))))


def _layer_norm(v, g, b):
    mu = jnp.mean(v, axis=-1, keepdims=True)
    xc = v - mu
    var = jnp.mean(xc * xc, axis=-1, keepdims=True)
    return xc * lax.rsqrt(var + EPS) * g + b


def _mm_kernel(x_ref, w_ref, o_ref):
    o_ref[...] = jnp.dot(x_ref[...], w_ref[...], preferred_element_type=F32).astype(o_ref.dtype)


def _matmul(x, w, tm, tn, name):
    m, k = x.shape
    n = w.shape[1]
    return pl.pallas_call(
        _mm_kernel,
        grid=(m // tm, n // tn),
        in_specs=[pl.BlockSpec((tm, k), lambda i, j: (i, 0)),
                  pl.BlockSpec((k, tn), lambda i, j: (0, j))],
        out_specs=pl.BlockSpec((tm, tn), lambda i, j: (i, j)),
        out_shape=jax.ShapeDtypeStruct((m, n), F32),
        compiler_params=_cparams("parallel", "parallel"),
        name=name,
    )(x, w)


def _conv_kernel(first_ref, last_ref, xp_ref, x_ref, xn_ref, w_ref, b_ref, o_ref, ext_ref, *, tm):
    i = pl.program_id(0)
    keep_prev = jnp.where(first_ref[i] == 1, 0.0, 1.0).astype(F32)
    keep_next = jnp.where(last_ref[i] == 1, 0.0, 1.0).astype(F32)
    ext_ref[0:SUBLANE, :] = xp_ref[...] * keep_prev
    ext_ref[SUBLANE:SUBLANE + tm, :] = x_ref[...]
    ext_ref[SUBLANE + tm:2 * SUBLANE + tm, :] = xn_ref[...] * keep_next
    acc = jnp.broadcast_to(b_ref[...], (tm, b_ref.shape[1]))
    for k in range(D_CONV):
        acc = acc + w_ref[k:k + 1, :] * ext_ref[pl.ds(SUBLANE - CONV_PAD + k, tm), :]
    o_ref[...] = _silu(acc)


def _conv_silu(xbc, conv_w, conv_b, first, last):
    t, c = xbc.shape
    tm, tc = CONV_TM, CONV_TC
    rb = tm // SUBLANE
    last_rb = t // SUBLANE - 1
    grid_spec = pltpu.PrefetchScalarGridSpec(
        num_scalar_prefetch=2,
        grid=(t // tm, c // tc),
        in_specs=[
            pl.BlockSpec((SUBLANE, tc), lambda i, j, f, l: (jnp.maximum(i * rb - 1, 0), j)),
            pl.BlockSpec((tm, tc), lambda i, j, f, l: (i, j)),
            pl.BlockSpec((SUBLANE, tc), lambda i, j, f, l: (jnp.minimum((i + 1) * rb, last_rb), j)),
            pl.BlockSpec((D_CONV, tc), lambda i, j, f, l: (0, j)),
            pl.BlockSpec((1, tc), lambda i, j, f, l: (0, j)),
        ],
        out_specs=pl.BlockSpec((tm, tc), lambda i, j, f, l: (i, j)),
        scratch_shapes=[pltpu.VMEM((tm + 2 * SUBLANE, tc), F32)],
    )
    return pl.pallas_call(
        functools.partial(_conv_kernel, tm=tm),
        grid_spec=grid_spec,
        out_shape=jax.ShapeDtypeStruct((t, c), F32),
        compiler_params=_cparams("parallel", "parallel"),
        name="conv_silu",
    )(first, last, xbc, xbc, xbc, conv_w, conv_b.reshape(1, c))


def _ssd_kernel(reset_ref, xs_ref, bm_ref, cm_ref, dt_ref, dtb_ref, alog_ref, *rest, rev):
    if rev:
        yf_ref, z_ref, dskip_ref, nw_ref, o_ref, h_ref, xdt_ref, xdec_ref, y_ref = rest
    else:
        o_ref, h_ref, xdt_ref, xdec_ref = rest
    c = CHUNK
    step = pl.program_id(0)

    @pl.when(reset_ref[step] == 1)
    def _():
        h_ref[...] = jnp.zeros_like(h_ref)

    dt = _softplus(dt_ref[...] + dtb_ref[...])
    a = dt * (-jnp.exp(alog_ref[...]))
    row = lax.broadcasted_iota(jnp.int32, (c, c), 0)
    col = lax.broadcasted_iota(jnp.int32, (c, c), 1)
    tri = (col < row) if rev else (col <= row)
    cum = jnp.dot(tri.astype(F32), a, precision=lax.Precision.HIGHEST,
                  preferred_element_type=F32)
    if rev:
        p = -cum
        total = cum[c - 1:c, :] + a[c - 1:c, :]
        dfs = jnp.exp(total + p)
        dte = jnp.exp(-p)
        mask = row <= col
    else:
        p = cum
        total = cum[c - 1:c, :]
        dfs = jnp.exp(p)
        dte = jnp.exp(total - p)
        mask = row >= col
    cdec = jnp.exp(total)
    p_t = p.T
    lo = N_HEADS if rev else 0
    left = col < HEAD_DIM

    def expand_pair(v, h0, rows):
        return jnp.where(left[:rows], jnp.broadcast_to(v[:, h0:h0 + 1], (rows, LANE)),
                         jnp.broadcast_to(v[:, h0 + 1:h0 + 2], (rows, LANE)))

    for j in range(N_HEADS // 2):
        sl = slice(j * LANE, (j + 1) * LANE)
        h0 = lo + 2 * j
        xdt = xs_ref[:, sl] * expand_pair(dt, h0, c)
        xdt_ref[:, sl] = xdt.astype(BF16)
        xdec_ref[:, sl] = (xdt * expand_pair(dte, h0, c)).astype(BF16)

    pairs = GROUP_W // LANE
    for g in range(N_GROUPS):
        gs = slice(g * D_STATE, (g + 1) * D_STATE)
        ws = slice(g * GROUP_W, (g + 1) * GROUP_W)
        bm_g = bm_ref[:, gs]
        cm_b = cm_ref[:, gs].astype(BF16)
        cb = lax.dot_general(cm_b, bm_g.astype(BF16), (((1,), (1,)), ((), ())),
                             preferred_element_type=F32)
        h_prev = h_ref[g]
        y_off = jnp.dot(cm_b, h_prev.astype(BF16), preferred_element_type=F32)
        chunk_state = jnp.dot(bm_g.T.astype(BF16), xdec_ref[:, ws], preferred_element_type=F32)
        cd_row = jnp.concatenate([expand_pair(cdec, lo + g * 2 * pairs + 2 * jj, 1)
                                  for jj in range(pairs)], axis=1)
        h_ref[g] = h_prev * cd_row + chunk_state
        for jj in range(pairs):
            h0 = lo + g * 2 * pairs + 2 * jj
            sl = slice(g * GROUP_W + jj * LANE, g * GROUP_W + (jj + 1) * LANE)
            x_pair = xdt_ref[:, sl]
            halves = []
            for h in (h0, h0 + 1):
                diff = jnp.broadcast_to(p[:, h:h + 1], (c, c)) - p_t[h:h + 1, :]
                decay = jnp.exp(jnp.where(mask, diff, -jnp.inf))
                halves.append(jnp.dot((cb * decay).astype(BF16), x_pair, preferred_element_type=F32))
            y_t = jnp.where(left, halves[0], halves[1])
            y_t = y_t + y_off[:, jj * LANE:(jj + 1) * LANE] * expand_pair(dfs, h0, c)
            if rev:
                y_t = yf_ref[:, sl] + y_t + xs_ref[:, sl] * dskip_ref[:, sl]
                y_ref[:, sl] = y_t * _silu(z_ref[:, sl])
            else:
                o_ref[:, sl] = y_t
        if rev:
            yg = y_ref[:, ws]
            ms = jnp.mean(yg * yg, axis=-1, keepdims=True)
            o_ref[:, ws] = (yg * lax.rsqrt(ms + EPS) * nw_ref[:, ws]).astype(o_ref.dtype)


def _ssd(xbc, dt_raw, dt_bias, a_log, reset, *, rev, extra=()):
    t = xbc.shape[0]
    nc = t // CHUNK
    if rev:
        cidx = lambda s, r: nc - 1 - s
    else:
        cidx = lambda s, r: s
    bm_blk = D_INNER // BC_W
    row = lambda w: pl.BlockSpec((1, w), lambda s, r: (0, 0))
    in_specs = [
        pl.BlockSpec((CHUNK, D_INNER), lambda s, r: (cidx(s, r), 0)),
        pl.BlockSpec((CHUNK, BC_W), lambda s, r: (cidx(s, r), bm_blk)),
        pl.BlockSpec((CHUNK, BC_W), lambda s, r: (cidx(s, r), bm_blk + 1)),
        pl.BlockSpec((CHUNK, LANE), lambda s, r: (cidx(s, r), 0)),
        row(LANE), row(LANE),
    ]
    scratch = [pltpu.VMEM((N_GROUPS, D_STATE, GROUP_W), F32),
               pltpu.VMEM((CHUNK, D_INNER), BF16),
               pltpu.VMEM((CHUNK, D_INNER), BF16)]
    args = [reset, xbc, xbc, xbc, dt_raw, dt_bias, a_log]
    if rev:
        in_specs += [pl.BlockSpec((CHUNK, D_INNER), lambda s, r: (cidx(s, r), 0)),
                     pl.BlockSpec((CHUNK, D_INNER), lambda s, r: (cidx(s, r), 0)),
                     row(D_INNER), row(D_INNER)]
        scratch.append(pltpu.VMEM((CHUNK, D_INNER), F32))
        args += list(extra)
        out_dtype = BF16
    else:
        out_dtype = F32
    grid_spec = pltpu.PrefetchScalarGridSpec(
        num_scalar_prefetch=1,
        grid=(nc,),
        in_specs=in_specs,
        out_specs=pl.BlockSpec((CHUNK, D_INNER), lambda s, r: (cidx(s, r), 0)),
        scratch_shapes=scratch,
    )
    return pl.pallas_call(
        functools.partial(_ssd_kernel, rev=rev),
        grid_spec=grid_spec,
        out_shape=jax.ShapeDtypeStruct((t, D_INNER), out_dtype),
        compiler_params=_cparams("arbitrary"),
        name="ssd_bwd" if rev else "ssd_fwd",
    )(*args)


def _gmlp_kernel(uv_ref, nw_ref, nb_ref, sw_ref, sb_ref, o_ref):
    v = _layer_norm(_gelu_tanh(uv_ref[:, D_GMLP:]), nw_ref[...], nb_ref[...])
    for g in range(GMLP_GROUPS):
        sl = slice(g * GMLP_GW, (g + 1) * GMLP_GW)
        v_mix = jnp.dot(sw_ref[g], v[:, sl].astype(BF16), preferred_element_type=F32) + sb_ref[:, sl]
        o_ref[:, sl] = (_gelu_tanh(uv_ref[:, sl]) * v_mix).astype(o_ref.dtype)


def _gmlp(uv, norm_w, norm_b, sgu_w, sgu_b_full):
    t = uv.shape[0]
    const = lambda shape: pl.BlockSpec(shape, lambda i: (0,) * len(shape))
    return pl.pallas_call(
        _gmlp_kernel,
        grid=(t // CHUNK,),
        in_specs=[pl.BlockSpec((CHUNK, 2 * D_GMLP), lambda i: (i, 0)),
                  const((1, D_GMLP)), const((1, D_GMLP)),
                  const((GMLP_GROUPS, CHUNK, CHUNK)), const((CHUNK, D_GMLP))],
        out_specs=pl.BlockSpec((CHUNK, D_GMLP), lambda i: (i, 0)),
        out_shape=jax.ShapeDtypeStruct((t, D_GMLP), BF16),
        compiler_params=_cparams("parallel"),
        name="gmlp",
    )(uv, norm_w.reshape(1, -1), norm_b.reshape(1, -1), sgu_w, sgu_b_full)


def _merge_kernel(ys_ref, yg_ref, wa_ref, wb_ref, ga_ref, gb_ref, o_ref):
    p_a = jnp.dot(ys_ref[...], wa_ref[...], preferred_element_type=F32)
    p_b = jnp.dot(yg_ref[...], wb_ref[...], preferred_element_type=F32)
    merged = jax.nn.sigmoid(ga_ref[...]) * p_a + jax.nn.sigmoid(gb_ref[...]) * p_b
    o_ref[...] = merged.astype(o_ref.dtype)


def _merge(y_ssd, y_gmlp, w_branch, gates):
    t = y_ssd.shape[0]
    tm, tn = MERGE_TM, MERGE_TN
    gb_off = D_MODEL // tn
    return pl.pallas_call(
        _merge_kernel,
        grid=(t // tm, D_MODEL // tn),
        in_specs=[pl.BlockSpec((tm, D_INNER), lambda i, j: (i, 0)),
                  pl.BlockSpec((tm, D_GMLP), lambda i, j: (i, 0)),
                  pl.BlockSpec((D_INNER, tn), lambda i, j: (0, j)),
                  pl.BlockSpec((D_GMLP, tn), lambda i, j: (D_INNER // D_GMLP, j)),
                  pl.BlockSpec((tm, tn), lambda i, j: (i, j)),
                  pl.BlockSpec((tm, tn), lambda i, j: (i, j + gb_off))],
        out_specs=pl.BlockSpec((tm, tn), lambda i, j: (i, j)),
        out_shape=jax.ShapeDtypeStruct((t, D_MODEL), BF16),
        compiler_params=_cparams("parallel", "parallel"),
        name="branch_merge",
    )(y_ssd, y_gmlp, w_branch, w_branch, gates, gates)


def _outproj_ln_kernel(m_ref, w_ref, x_ref, g_ref, b_ref, o_ref, ob_ref):
    mix = jnp.dot(m_ref[...], w_ref[...], preferred_element_type=F32)
    y = _layer_norm(ALPHA * x_ref[...] + mix, g_ref[...], b_ref[...])
    o_ref[...] = y
    ob_ref[...] = y.astype(BF16)


def _outproj_ln(merged, w_out, x, g, b):
    t = x.shape[0]
    tm = LN_TM
    rowspec = pl.BlockSpec((tm, D_MODEL), lambda i: (i, 0))
    const = lambda shape: pl.BlockSpec(shape, lambda i: (0, 0))
    return pl.pallas_call(
        _outproj_ln_kernel,
        grid=(t // tm,),
        in_specs=[rowspec, const((D_MODEL, D_MODEL)), rowspec, const((1, D_MODEL)), const((1, D_MODEL))],
        out_specs=[rowspec, rowspec],
        out_shape=[jax.ShapeDtypeStruct((t, D_MODEL), F32), jax.ShapeDtypeStruct((t, D_MODEL), BF16)],
        compiler_params=_cparams("parallel"),
        name="outproj_ln1",
    )(merged, w_out, x, g.reshape(1, -1), b.reshape(1, -1))


def _router_kernel(x_ref, w_ref, b_ref, o_ref):
    lg = jnp.dot(x_ref[...], w_ref[...], precision=lax.Precision.HIGHEST,
                 preferred_element_type=F32) + b_ref[...]
    lane = lax.broadcasted_iota(jnp.int32, lg.shape, 1)
    big = jnp.int32(ROUTE_W)
    neg = -jnp.inf
    g_l = jnp.where(lane < N_EGROUPS, lg, neg)
    g_max = jnp.max(g_l, axis=-1, keepdims=True)
    g_sum = jnp.sum(jnp.exp(g_l - g_max), axis=-1, keepdims=True)
    g_w = 1.0 / g_sum
    g_idx = jnp.min(jnp.where(g_l == g_max, lane, big), axis=-1, keepdims=True)
    e_lo = N_EGROUPS + g_idx * E_PER_GROUP
    e_l = jnp.where(lane >= e_lo, jnp.where(lane < e_lo + E_PER_GROUP, lg, neg), neg)
    e_max = jnp.max(e_l, axis=-1, keepdims=True)
    e_sum = jnp.sum(jnp.exp(e_l - e_max), axis=-1, keepdims=True)
    i1 = jnp.min(jnp.where(e_l == e_max, lane, big), axis=-1, keepdims=True)
    e_l2 = jnp.where(lane == i1, neg, e_l)
    e_max2 = jnp.max(e_l2, axis=-1, keepdims=True)
    i2 = jnp.min(jnp.where(e_l2 == e_max2, lane, big), axis=-1, keepdims=True)
    p1 = 1.0 / e_sum
    p2 = jnp.exp(e_max2 - e_max) / e_sum
    w1 = g_w * (p1 / (p1 + p2))
    w2 = g_w * (p2 / (p1 + p2))
    id1 = (i1 - N_EGROUPS).astype(F32)
    id2 = (i2 - N_EGROUPS).astype(F32)
    out = jnp.where(lane == 0, id1, jnp.where(lane == 1, id2, jnp.where(lane == 2, w1, jnp.where(lane == 3, w2, 0.0))))
    o_ref[...] = out


def _router(x, w_r, b_r):
    t = x.shape[0]
    tm = ROUTE_TM
    return pl.pallas_call(
        _router_kernel,
        grid=(t // tm,),
        in_specs=[pl.BlockSpec((tm, D_MODEL), lambda i: (i, 0)),
                  pl.BlockSpec((D_MODEL, ROUTE_W), lambda i: (0, 0)),
                  pl.BlockSpec((1, ROUTE_W), lambda i: (0, 0))],
        out_specs=pl.BlockSpec((tm, ROUTE_W), lambda i: (i, 0)),
        out_shape=jax.ShapeDtypeStruct((t, ROUTE_W), F32),
        compiler_params=_cparams("parallel"),
        name="router",
    )(x, w_r, b_r)


def _row_gather(src_hbm, dst, sem, idx_ref, base, n_rows):
    def body(r, carry):
        tok = idx_ref[base + r]
        pltpu.make_async_copy(src_hbm.at[pl.ds(tok, 1), :], dst.at[pl.ds(r, 1), :], sem).start()
        return carry
    lax.fori_loop(0, n_rows, body, 0)


def _row_gather_wait(src_hbm, dst, sem, n_rows):
    def body(r, carry):
        pltpu.make_async_copy(src_hbm.at[pl.ds(0, 1), :], dst.at[pl.ds(r, 1), :], sem).wait()
        return carry
    lax.fori_loop(0, n_rows, body, 0)


def _expert_kernel(be_ref, tok_ref, nused_ref, x_hbm, wg_ref, wu_ref, wd_ref, o_ref, buf, sem):
    del be_ref
    b = pl.program_id(0)
    nb = pl.num_programs(0)
    slot = b % 2

    @pl.when(b == 0)
    def _():
        _row_gather(x_hbm, buf.at[0], sem.at[0], tok_ref, 0, EXPERT_BLK)

    @pl.when(b + 1 < nb)
    def _():
        _row_gather(x_hbm, buf.at[1 - slot], sem.at[1 - slot], tok_ref, (b + 1) * EXPERT_BLK, EXPERT_BLK)

    _row_gather_wait(x_hbm, buf.at[slot], sem.at[slot], EXPERT_BLK)

    @pl.when(b < nused_ref[0])
    def _():
        x = buf[slot].astype(BF16)
        h = _silu(jnp.dot(x, wg_ref[...], preferred_element_type=F32)) * \
            jnp.dot(x, wu_ref[...], preferred_element_type=F32)
        o_ref[...] = jnp.dot(h.astype(BF16), wd_ref[...], preferred_element_type=F32)

    @pl.when(b >= nused_ref[0])
    def _():
        o_ref[...] = jnp.zeros_like(o_ref)


def _experts(x, w_gate, w_up, w_down, block_expert, slot_tok, n_used):
    nb = block_expert.shape[0]
    grid_spec = pltpu.PrefetchScalarGridSpec(
        num_scalar_prefetch=3,
        grid=(nb,),
        in_specs=[
            pl.BlockSpec(memory_space=pl.ANY),
            pl.BlockSpec((None, D_MODEL, D_EXPERT), lambda b, be, tk, nu: (be[b], 0, 0)),
            pl.BlockSpec((None, D_MODEL, D_EXPERT), lambda b, be, tk, nu: (be[b], 0, 0)),
            pl.BlockSpec((None, D_EXPERT, D_MODEL), lambda b, be, tk, nu: (be[b], 0, 0)),
        ],
        out_specs=pl.BlockSpec((EXPERT_BLK, D_MODEL), lambda b, be, tk, nu: (b, 0)),
        scratch_shapes=[pltpu.VMEM((2, EXPERT_BLK, D_MODEL), F32), pltpu.SemaphoreType.DMA((2,))],
    )
    return pl.pallas_call(
        _expert_kernel,
        grid_spec=grid_spec,
        out_shape=jax.ShapeDtypeStruct((nb * EXPERT_BLK, D_MODEL), F32),
        compiler_params=_cparams("arbitrary"),
        name="experts",
    )(block_expert, slot_tok, n_used, x, w_gate, w_up, w_down)


def _combine_ln_kernel(dest_ref, yb_hbm, route_ref, x_ref, g_ref, b_ref, o_ref, ob_ref, buf, sem, *, tm):
    i = pl.program_id(0)
    n = pl.num_programs(0)
    slot = i % 2
    rows = TOP_K * tm

    @pl.when(i == 0)
    def _():
        _row_gather(yb_hbm, buf.at[0], sem.at[0], dest_ref, 0, rows)

    @pl.when(i + 1 < n)
    def _():
        _row_gather(yb_hbm, buf.at[1 - slot], sem.at[1 - slot], dest_ref, (i + 1) * rows, rows)

    _row_gather_wait(yb_hbm, buf.at[slot], sem.at[slot], rows)

    route = route_ref[...]
    ffn = buf[slot, 0:tm, :] * route[:, 2:3] + buf[slot, tm:rows, :] * route[:, 3:4]
    y = _layer_norm(ALPHA * x_ref[...] + ffn, g_ref[...], b_ref[...])
    o_ref[...] = y
    ob_ref[...] = y.astype(BF16)


def _combine_ln(yb, dest_tiles, route, x, g, b):
    t = x.shape[0]
    tm = LN_TM
    rowspec = pl.BlockSpec((tm, D_MODEL), lambda i, d: (i, 0))
    const = pl.BlockSpec((1, D_MODEL), lambda i, d: (0, 0))
    grid_spec = pltpu.PrefetchScalarGridSpec(
        num_scalar_prefetch=1,
        grid=(t // tm,),
        in_specs=[pl.BlockSpec(memory_space=pl.ANY),
                  pl.BlockSpec((tm, ROUTE_W), lambda i, d: (i, 0)),
                  rowspec, const, const],
        out_specs=[rowspec, rowspec],
        scratch_shapes=[pltpu.VMEM((2, TOP_K * tm, D_MODEL), F32), pltpu.SemaphoreType.DMA((2,))],
    )
    return pl.pallas_call(
        functools.partial(_combine_ln_kernel, tm=tm),
        grid_spec=grid_spec,
        out_shape=[jax.ShapeDtypeStruct((t, D_MODEL), F32), jax.ShapeDtypeStruct((t, D_MODEL), BF16)],
        compiler_params=_cparams("arbitrary"),
        name="combine_ln2",
    )(dest_tiles, yb, route, x, g.reshape(1, -1), b.reshape(1, -1))


def _routing_tables(route, t):
    expert = route[:, 0:TOP_K].astype(jnp.int32)
    a_n = t * TOP_K
    e_flat = expert.reshape(a_n)
    order = jnp.argsort(e_flat)
    e_sorted = e_flat[order]
    counts = jnp.bincount(e_flat, length=N_EXPERTS)
    padded = (counts + EXPERT_BLK - 1) // EXPERT_BLK * EXPERT_BLK
    starts = jnp.cumsum(counts) - counts
    pends = jnp.cumsum(padded)
    pstarts = pends - padded
    dest_sorted = (pstarts[e_sorted] + (jnp.arange(a_n) - starts[e_sorted])).astype(jnp.int32)
    n_blocks = a_n // EXPERT_BLK + N_EXPERTS
    slot_tok = jnp.zeros((n_blocks * EXPERT_BLK,), jnp.int32).at[dest_sorted].set((order // TOP_K).astype(jnp.int32))
    dest = jnp.zeros((a_n,), jnp.int32).at[order].set(dest_sorted).reshape(t, TOP_K)
    block_expert = jnp.clip(jnp.searchsorted(pends, jnp.arange(n_blocks) * EXPERT_BLK, side='right'),
                            0, N_EXPERTS - 1).astype(jnp.int32)
    n_used = (pends[-1:] // EXPERT_BLK).astype(jnp.int32)
    dest_tiles = dest.reshape(t // LN_TM, LN_TM, TOP_K).transpose(0, 2, 1).reshape(a_n)
    return slot_tok, dest_tiles, block_expert, n_used


def _seq_flags(groups, tile):
    first, last = [], []
    for batch, length in groups:
        n = length // tile
        for _ in range(batch):
            first += [1] + [0] * (n - 1)
            last += [0] * (n - 1) + [1]
    return np.asarray(first, np.int32), np.asarray(last, np.int32)


def _trunk(x, xb, groups, w_in, conv_w, conv_b, dt_bias, a_log, d_skip, ssd_norm_w, sgu_norm_w, sgu_norm_b,
           sgu_w, sgu_b, w_branch, w_out, ln1_g, ln1_b, ln2_g, ln2_b,
           w_router_group, b_router_group, w_router_expert, b_router_expert, w_gate, w_up, w_down):
    t = x.shape[0]
    depth = w_in.shape[0]
    conv_first, conv_last = _seq_flags(groups, CONV_TM)
    chunk_first, chunk_last = _seq_flags(groups, CHUNK)
    reset_f = jnp.asarray(chunk_first)
    reset_b = jnp.asarray(chunk_last[::-1].copy())
    conv_first, conv_last = jnp.asarray(conv_first), jnp.asarray(conv_last)
    pad_r = ROUTE_W - N_EGROUPS - N_EXPERTS
    for i in range(depth):
        w = w_in[i]
        w_z = w[:, :OFF_XBC].astype(BF16)
        w_xbc = w[:, OFF_XBC:OFF_DT].astype(BF16)
        w_dt = w[:, OFF_DT:OFF_UV].astype(BF16)
        w_uv = w[:, OFF_UV:OFF_GATE].astype(BF16)
        w_g = w[:, OFF_GATE:].astype(BF16)
        z = _matmul(xb, w_z, MM_TM, MM_TN, "proj_z")
        xbc = _matmul(xb, w_xbc, MM_TM, MM_TN, "proj_xbc")
        dt_raw = _matmul(xb, w_dt, MM_TM, LANE, "proj_dt")
        uv = _matmul(xb, w_uv, MM_TM, MM_TN, "proj_uv")
        gates = _matmul(xb, w_g, MM_TM, MM_TN, "proj_gates")

        xbc = _conv_silu(xbc, conv_w[i], conv_b[i], conv_first, conv_last)
        dtb = dt_bias[i].reshape(1, 2 * N_HEADS)
        alog = a_log[i].reshape(1, 2 * N_HEADS)
        y_f = _ssd(xbc, dt_raw, dtb, alog, reset_f, rev=False)
        dskip = jnp.repeat(d_skip[i], HEAD_DIM).reshape(1, D_INNER)
        y_ssd = _ssd(xbc, dt_raw, dtb, alog, reset_b, rev=True,
                     extra=(y_f, z, dskip, ssd_norm_w[i].reshape(1, D_INNER)))

        sgu_b_full = jnp.repeat(sgu_b[i].T, GMLP_GW, axis=1)
        y_gmlp = _gmlp(uv, sgu_norm_w[i], sgu_norm_b[i], sgu_w[i].astype(BF16), sgu_b_full)

        merged = _merge(y_ssd, y_gmlp, w_branch[i].astype(BF16), gates)
        x, xb = _outproj_ln(merged, w_out[i].astype(BF16), x, ln1_g[i], ln1_b[i])

        w_r = jnp.concatenate([w_router_group[i], w_router_expert[i],
                               jnp.zeros((D_MODEL, pad_r), F32)], axis=1)
        b_r = jnp.concatenate([b_router_group[i], b_router_expert[i], jnp.zeros((pad_r,), F32)]).reshape(1, ROUTE_W)
        route = _router(x, w_r, b_r)
        slot_tok, dest_tiles, block_expert, n_used = _routing_tables(route, t)
        yb = _experts(x, w_gate[i].astype(BF16), w_up[i].astype(BF16), w_down[i].astype(BF16),
                      block_expert, slot_tok, n_used)
        x, xb = _combine_ln(yb, dest_tiles, route, x, ln2_g[i], ln2_b[i])
    return x


def kernel(x_prompt, x_sample, w_in, conv_w, conv_b, dt_bias, a_log, d_skip, ssd_norm_w, sgu_norm_w, sgu_norm_b, sgu_w, sgu_b, w_branch, w_out, ln1_g, ln1_b, ln2_g, ln2_b, w_router_group, b_router_group, w_router_expert, b_router_expert, w_gate, w_up, w_down):
    groups = (x_prompt.shape[:2], x_sample.shape[:2])
    for _, length in groups:
        assert length % CONV_TM == 0 and length % CHUNK == 0
    x = jnp.concatenate([x_prompt.reshape(-1, D_MODEL), x_sample.reshape(-1, D_MODEL)], axis=0)
    t = x.shape[0]
    assert t % MM_TM == 0 and (t * TOP_K) % EXPERT_BLK == 0
    y = _trunk(x, x.astype(BF16), groups, w_in, conv_w, conv_b, dt_bias, a_log, d_skip, ssd_norm_w,
               sgu_norm_w, sgu_norm_b, sgu_w, sgu_b, w_branch, w_out, ln1_g, ln1_b, ln2_g, ln2_b,
               w_router_group, b_router_group, w_router_expert, b_router_expert, w_gate, w_up, w_down)
    n_p = x_prompt.shape[0] * x_prompt.shape[1]
    return (y[:n_p].reshape(x_prompt.shape), y[n_p:].reshape(x_sample.shape))
```

```python
import functools
import math

import numpy as np
import jax
import jax.numpy as jnp
from jax import lax
from jax.experimental import pallas as pl
from jax.experimental.pallas import tpu as pltpu

F32 = jnp.float32
BF16 = jnp.bfloat16

D_MODEL = 2048
DEPTH = 4
D_INNER = 2 * D_MODEL
HEAD_DIM = 64
N_HEADS = D_INNER // HEAD_DIM
N_GROUPS = 8
GROUP_W = D_INNER // N_GROUPS
D_STATE = 128
D_CONV = 5
CONV_PAD = D_CONV // 2
BC_W = N_GROUPS * D_STATE
CONV_DIM = D_INNER + 2 * BC_W
CHUNK = 128
D_GMLP = D_MODEL
GMLP_GROUPS = 8
GMLP_GW = D_GMLP // GMLP_GROUPS
OFF_XBC = D_INNER
OFF_DT = OFF_XBC + CONV_DIM
OFF_UV = OFF_DT + 2 * N_HEADS
OFF_GATE = OFF_UV + 2 * D_GMLP
N_IN = OFF_GATE + 2 * D_MODEL
N_EGROUPS = 8
E_PER_GROUP = 8
N_EXPERTS = N_EGROUPS * E_PER_GROUP
TOP_K = 2
D_EXPERT = D_MODEL // 4
ALPHA = (2.0 * DEPTH) ** 0.25
EPS = 1e-5

LANE = 128
SUBLANE = 8
V7X_VMEM_BYTES = 64 * 1024 * 1024
VMEM_LIMIT = V7X_VMEM_BYTES * 3 // 4

MM_TM = 1024
MM_TN = 1024
CONV_TM = 512
CONV_TC = 512
PRE_CHUNKS = 8
MERGE_TM = 512
MERGE_TN = 512
LN_TM = 256
ROUTE_TM = 512
EXPERT_BLK = 256
GATHER_UNROLL = 8
ROUTE_W = LANE


def _cparams(*sem, **kw):
    return pltpu.CompilerParams(dimension_semantics=sem, vmem_limit_bytes=VMEM_LIMIT, **kw)


def _silu(x):
    return x * jax.nn.sigmoid(x)


def _softplus(x):
    return jnp.maximum(x, 0.0) + jnp.log1p(jnp.exp(-jnp.abs(x)))


def _gelu_tanh(x):
    c = math.sqrt(2.0 / math.pi)
    return x * (0.5 * (1.0 + jnp.tanh(c * (x + 0.044715 * (x * x * x)))))


def _layer_norm(v, g, b):
    mu = jnp.mean(v, axis=-1, keepdims=True)
    xc = v - mu
    var = jnp.mean(xc * xc, axis=-1, keepdims=True)
    return xc * lax.rsqrt(var + EPS) * g + b


def _proj_kernel(x_ref, w_hbm, o_ref, w_f32, w_bf, sem, *, layer, col_off, tn):
    @pl.when(pl.program_id(1) == 0)
    def _():
        col = pl.multiple_of(col_off + pl.program_id(0) * tn, LANE)
        copy = pltpu.make_async_copy(w_hbm.at[layer, :, pl.ds(col, tn)], w_f32, sem)
        copy.start()
        copy.wait()
        w_bf[...] = w_f32[...].astype(BF16)

    o_ref[...] = jnp.dot(x_ref[...], w_bf[...], preferred_element_type=F32)


def _proj(x, w_in, layer, col_off, n, tm, tn, name):
    m, k = x.shape
    return pl.pallas_call(
        functools.partial(_proj_kernel, layer=layer, col_off=col_off, tn=tn),
        grid=(n // tn, m // tm),
        in_specs=[pl.BlockSpec((tm, k), lambda j, i: (i, 0)),
                  pl.BlockSpec(memory_space=pl.ANY)],
        out_specs=pl.BlockSpec((tm, tn), lambda j, i: (i, j)),
        out_shape=jax.ShapeDtypeStruct((m, n), F32),
        scratch_shapes=[pltpu.VMEM((k, tn), F32), pltpu.VMEM((k, tn), BF16), pltpu.SemaphoreType.DMA],
        compiler_params=_cparams("arbitrary", "arbitrary"),
        name=name,
    )(x, w_in)


def _conv_kernel(first_ref, last_ref, xp_ref, x_ref, xn_ref, w_ref, b_ref, o_ref, ext_ref, *, tm):
    i = pl.program_id(0)
    keep_prev = jnp.where(first_ref[i] == 1, 0.0, 1.0).astype(F32)
    keep_next = jnp.where(last_ref[i] == 1, 0.0, 1.0).astype(F32)
    ext_ref[0:SUBLANE, :] = xp_ref[...] * keep_prev
    ext_ref[SUBLANE:SUBLANE + tm, :] = x_ref[...]
    ext_ref[SUBLANE + tm:2 * SUBLANE + tm, :] = xn_ref[...] * keep_next
    acc = jnp.broadcast_to(b_ref[...], (tm, b_ref.shape[1]))
    for k in range(D_CONV):
        acc = acc + w_ref[k:k + 1, :] * ext_ref[pl.ds(SUBLANE - CONV_PAD + k, tm), :]
    o_ref[...] = _silu(acc)


def _conv_silu(xbc, conv_w, conv_b, first, last):
    t, c = xbc.shape
    tm, tc = CONV_TM, CONV_TC
    rb = tm // SUBLANE
    last_rb = t // SUBLANE - 1
    grid_spec = pltpu.PrefetchScalarGridSpec(
        num_scalar_prefetch=2,
        grid=(t // tm, c // tc),
        in_specs=[
            pl.BlockSpec((SUBLANE, tc), lambda i, j, f, l: (jnp.maximum(i * rb - 1, 0), j)),
            pl.BlockSpec((tm, tc), lambda i, j, f, l: (i, j)),
            pl.BlockSpec((SUBLANE, tc), lambda i, j, f, l: (jnp.minimum((i + 1) * rb, last_rb), j)),
            pl.BlockSpec((D_CONV, tc), lambda i, j, f, l: (0, j)),
            pl.BlockSpec((1, tc), lambda i, j, f, l: (0, j)),
        ],
        out_specs=pl.BlockSpec((tm, tc), lambda i, j, f, l: (i, j)),
        scratch_shapes=[pltpu.VMEM((tm + 2 * SUBLANE, tc), F32)],
    )
    return pl.pallas_call(
        functools.partial(_conv_kernel, tm=tm),
        grid_spec=grid_spec,
        out_shape=jax.ShapeDtypeStruct((t, c), F32),
        compiler_params=_cparams("parallel", "parallel"),
        name="conv_silu",
    )(first, last, xbc, xbc, xbc, conv_w, conv_b.reshape(1, c))


def _ssd_pre_kernel(dt_ref, dtb_ref, alog_ref, dtt_ref, p_ref, pt_ref, w_ref, rows_ref):
    c = CHUNK
    row = lax.broadcasted_iota(jnp.int32, (c, c), 0)
    col = lax.broadcasted_iota(jnp.int32, (c, c), 1)
    tri = (col <= row).astype(F32)
    fwd = col < N_HEADS
    neg_a_rate = -jnp.exp(alog_ref[...])
    for k in range(PRE_CHUNKS):
        rs = slice(k * c, (k + 1) * c)
        dt = _softplus(dt_ref[rs, :] + dtb_ref[...])
        a = dt * neg_a_rate
        cum = jnp.dot(tri, a, precision=lax.Precision.HIGHEST, preferred_element_type=F32)
        total = cum[c - 1:c, :]
        p = jnp.where(fwd, cum, a - cum)
        dte = jnp.exp(jnp.where(fwd, total - p, -p))
        dtt_ref[rs, :] = dt.T
        p_ref[rs, :] = p
        pt_ref[rs, :] = p.T
        w_ref[rs, :] = dt * dte
        sub = lax.broadcasted_iota(jnp.int32, (SUBLANE, c), 0)
        off = jnp.where(fwd[:1], 0.0, total)
        rows_ref[k * SUBLANE:(k + 1) * SUBLANE, :] = jnp.where(
            sub == 0, off, jnp.where(sub == 1, jnp.exp(total), 0.0))


def _ssd_pre(dt_raw, dt_bias, a_log):
    t = dt_raw.shape[0]
    rows = PRE_CHUNKS * CHUNK
    blk = pl.BlockSpec((rows, LANE), lambda i: (i, 0))
    const = pl.BlockSpec((1, LANE), lambda i: (0, 0))
    tok = jax.ShapeDtypeStruct((t, LANE), F32)
    return pl.pallas_call(
        _ssd_pre_kernel,
        grid=(t // rows,),
        in_specs=[blk, const, const],
        out_specs=[blk, blk, blk, blk, pl.BlockSpec((PRE_CHUNKS * SUBLANE, LANE), lambda i: (i, 0))],
        out_shape=[tok, tok, tok, tok, jax.ShapeDtypeStruct((t // CHUNK * SUBLANE, LANE), F32)],
        compiler_params=_cparams("parallel"),
        name="ssd_pre",
    )(dt_raw, dt_bias, a_log)


def _ssd_kernel(reset_ref, xs_ref, bm_ref, cm_ref, dtt_ref, p_ref, pt_ref, w_ref, rows_ref, *rest, rev):
    if rev:
        yf_ref, z_ref, dskip_ref, nw_ref, o_ref, h_ref, xdec_ref, y_ref = rest
    else:
        o_ref, h_ref, xdec_ref = rest
    c = CHUNK
    step = pl.program_id(0)

    @pl.when(reset_ref[step] == 1)
    def _():
        h_ref[...] = jnp.zeros_like(h_ref)

    row = lax.broadcasted_iota(jnp.int32, (c, c), 0)
    col = lax.broadcasted_iota(jnp.int32, (c, c), 1)
    mask = (row <= col) if rev else (row >= col)
    left = col < HEAD_DIM
    lo = N_HEADS if rev else 0
    p = p_ref[...]
    w = w_ref[...]
    rows = rows_ref[...]
    off = rows[0:1, :]
    cdec = rows[1:2, :]

    def expand_pair(v, h0, n_rows):
        return jnp.where(left[:n_rows], jnp.broadcast_to(v[:, h0:h0 + 1], (n_rows, LANE)),
                         jnp.broadcast_to(v[:, h0 + 1:h0 + 2], (n_rows, LANE)))

    pairs = GROUP_W // LANE
    for g in range(N_GROUPS):
        for jj in range(pairs):
            j = g * pairs + jj
            sl = slice(j * LANE, (j + 1) * LANE)
            xdec_ref[:, sl] = (xs_ref[:, sl] * expand_pair(w, lo + 2 * j, c)).astype(BF16)
        gs = slice(g * D_STATE, (g + 1) * D_STATE)
        ws = slice(g * GROUP_W, (g + 1) * GROUP_W)
        bm_g = bm_ref[:, gs]
        cm_g = cm_ref[:, gs]
        cb = lax.dot_general(cm_g.astype(BF16), bm_g.astype(BF16), (((1,), (1,)), ((), ())),
                             preferred_element_type=F32)
        h_prev = h_ref[g]
        for jj in range(pairs):
            h0 = lo + g * 2 * pairs + 2 * jj
            sl = slice(g * GROUP_W + jj * LANE, g * GROUP_W + (jj + 1) * LANE)
            rhs = jnp.concatenate([xs_ref[:, sl].astype(BF16),
                                   h_prev[:, jj * LANE:(jj + 1) * LANE].astype(BF16)], axis=0)
            halves = []
            for h in (h0, h0 + 1):
                p_col = jnp.broadcast_to(p[:, h:h + 1], (c, c))
                decay = jnp.exp(jnp.where(mask, p_col - pt_ref[h:h + 1, :], -jnp.inf))
                scores = cb * decay * dtt_ref[h:h + 1, :]
                from_state = cm_g * jnp.exp(p_col + off[:, h:h + 1])
                lhs = jnp.concatenate([scores.astype(BF16), from_state.astype(BF16)], axis=1)
                halves.append(jnp.dot(lhs, rhs, preferred_element_type=F32))
            y_t = jnp.where(left, halves[0], halves[1])
            if rev:
                y_t = yf_ref[:, sl] + y_t + xs_ref[:, sl] * dskip_ref[:, sl]
                y_ref[:, sl] = y_t * _silu(z_ref[:, sl])
            else:
                o_ref[:, sl] = y_t
        chunk_state = jnp.dot(bm_g.T.astype(BF16), xdec_ref[:, ws], preferred_element_type=F32)
        cd_row = jnp.concatenate([expand_pair(cdec, lo + g * 2 * pairs + 2 * jj, 1)
                                  for jj in range(pairs)], axis=1)
        h_ref[g] = h_prev * cd_row + chunk_state
        if rev:
            yg = y_ref[:, ws]
            ms = jnp.mean(yg * yg, axis=-1, keepdims=True)
            o_ref[:, ws] = (yg * lax.rsqrt(ms + EPS) * nw_ref[:, ws]).astype(o_ref.dtype)


def _ssd(xbc, pre, reset, *, rev, extra=()):
    t = xbc.shape[0]
    nc = t // CHUNK
    if rev:
        cidx = lambda s: nc - 1 - s
    else:
        cidx = lambda s: s
    bm_blk = D_INNER // BC_W
    row = lambda w: pl.BlockSpec((1, w), lambda s, r: (0, 0))
    tok = pl.BlockSpec((CHUNK, LANE), lambda s, r: (cidx(s), 0))
    wide = pl.BlockSpec((CHUNK, D_INNER), lambda s, r: (cidx(s), 0))
    in_specs = [
        wide,
        pl.BlockSpec((CHUNK, BC_W), lambda s, r: (cidx(s), bm_blk)),
        pl.BlockSpec((CHUNK, BC_W), lambda s, r: (cidx(s), bm_blk + 1)),
        tok, tok, tok, tok,
        pl.BlockSpec((SUBLANE, LANE), lambda s, r: (cidx(s), 0)),
    ]
    scratch = [pltpu.VMEM((N_GROUPS, D_STATE, GROUP_W), F32),
               pltpu.VMEM((CHUNK, D_INNER), BF16)]
    args = [reset, xbc, xbc, xbc, *pre]
    if rev:
        in_specs += [wide, wide, row(D_INNER), row(D_INNER)]
        scratch.append(pltpu.VMEM((CHUNK, D_INNER), F32))
        args += list(extra)
        out_dtype = BF16
    else:
        out_dtype = F32
    grid_spec = pltpu.PrefetchScalarGridSpec(
        num_scalar_prefetch=1,
        grid=(nc,),
        in_specs=in_specs,
        out_specs=wide,
        scratch_shapes=scratch,
    )
    return pl.pallas_call(
        functools.partial(_ssd_kernel, rev=rev),
        grid_spec=grid_spec,
        out_shape=jax.ShapeDtypeStruct((t, D_INNER), out_dtype),
        compiler_params=_cparams("arbitrary"),
        name="ssd_bwd" if rev else "ssd_fwd",
    )(*args)


def _gmlp_kernel(uv_ref, nw_ref, nb_ref, sw_ref, sb_ref, o_ref):
    v = _layer_norm(_gelu_tanh(uv_ref[:, D_GMLP:]), nw_ref[...], nb_ref[...])
    for g in range(GMLP_GROUPS):
        sl = slice(g * GMLP_GW, (g + 1) * GMLP_GW)
        v_mix = jnp.dot(sw_ref[g], v[:, sl].astype(BF16), preferred_element_type=F32) + sb_ref[:, sl]
        o_ref[:, sl] = (_gelu_tanh(uv_ref[:, sl]) * v_mix).astype(o_ref.dtype)


def _gmlp(uv, norm_w, norm_b, sgu_w, sgu_b_full):
    t = uv.shape[0]
    const = lambda shape: pl.BlockSpec(shape, lambda i: (0,) * len(shape))
    return pl.pallas_call(
        _gmlp_kernel,
        grid=(t // CHUNK,),
        in_specs=[pl.BlockSpec((CHUNK, 2 * D_GMLP), lambda i: (i, 0)),
                  const((1, D_GMLP)), const((1, D_GMLP)),
                  const((GMLP_GROUPS, CHUNK, CHUNK)), const((CHUNK, D_GMLP))],
        out_specs=pl.BlockSpec((CHUNK, D_GMLP), lambda i: (i, 0)),
        out_shape=jax.ShapeDtypeStruct((t, D_GMLP), BF16),
        compiler_params=_cparams("parallel"),
        name="gmlp",
    )(uv, norm_w.reshape(1, -1), norm_b.reshape(1, -1), sgu_w, sgu_b_full)


def _merge_kernel(ys_ref, yg_ref, wa_ref, wb_ref, ga_ref, gb_ref, o_ref):
    p_a = jnp.dot(ys_ref[...], wa_ref[...], preferred_element_type=F32)
    p_b = jnp.dot(yg_ref[...], wb_ref[...], preferred_element_type=F32)
    merged = jax.nn.sigmoid(ga_ref[...]) * p_a + jax.nn.sigmoid(gb_ref[...]) * p_b
    o_ref[...] = merged.astype(o_ref.dtype)


def _merge(y_ssd, y_gmlp, w_branch, gates):
    t = y_ssd.shape[0]
    tm, tn = MERGE_TM, MERGE_TN
    gb_off = D_MODEL // tn
    return pl.pallas_call(
        _merge_kernel,
        grid=(t // tm, D_MODEL // tn),
        in_specs=[pl.BlockSpec((tm, D_INNER), lambda i, j: (i, 0)),
                  pl.BlockSpec((tm, D_GMLP), lambda i, j: (i, 0)),
                  pl.BlockSpec((D_INNER, tn), lambda i, j: (0, j)),
                  pl.BlockSpec((D_GMLP, tn), lambda i, j: (D_INNER // D_GMLP, j)),
                  pl.BlockSpec((tm, tn), lambda i, j: (i, j)),
                  pl.BlockSpec((tm, tn), lambda i, j: (i, j + gb_off))],
        out_specs=pl.BlockSpec((tm, tn), lambda i, j: (i, j)),
        out_shape=jax.ShapeDtypeStruct((t, D_MODEL), BF16),
        compiler_params=_cparams("parallel", "parallel"),
        name="branch_merge",
    )(y_ssd, y_gmlp, w_branch, w_branch, gates, gates)


def _outproj_ln_kernel(m_ref, w_ref, x_ref, g_ref, b_ref, o_ref, ob_ref):
    mix = jnp.dot(m_ref[...], w_ref[...], preferred_element_type=F32)
    y = _layer_norm(ALPHA * x_ref[...] + mix, g_ref[...], b_ref[...])
    o_ref[...] = y
    ob_ref[...] = y.astype(BF16)


def _outproj_ln(merged, w_out, x, g, b):
    t = x.shape[0]
    tm = LN_TM
    rowspec = pl.BlockSpec((tm, D_MODEL), lambda i: (i, 0))
    const = lambda shape: pl.BlockSpec(shape, lambda i: (0, 0))
    return pl.pallas_call(
        _outproj_ln_kernel,
        grid=(t // tm,),
        in_specs=[rowspec, const((D_MODEL, D_MODEL)), rowspec, const((1, D_MODEL)), const((1, D_MODEL))],
        out_specs=[rowspec, rowspec],
        out_shape=[jax.ShapeDtypeStruct((t, D_MODEL), F32), jax.ShapeDtypeStruct((t, D_MODEL), BF16)],
        compiler_params=_cparams("parallel"),
        name="outproj_ln1",
    )(merged, w_out, x, g.reshape(1, -1), b.reshape(1, -1))


def _router_kernel(x_ref, w_ref, b_ref, o_ref):
    lg = jnp.dot(x_ref[...], w_ref[...], precision=lax.Precision.HIGHEST,
                 preferred_element_type=F32) + b_ref[...]
    lane = lax.broadcasted_iota(jnp.int32, lg.shape, 1)
    big = jnp.int32(ROUTE_W)
    neg = -jnp.inf
    g_l = jnp.where(lane < N_EGROUPS, lg, neg)
    g_max = jnp.max(g_l, axis=-1, keepdims=True)
    g_sum = jnp.sum(jnp.exp(g_l - g_max), axis=-1, keepdims=True)
    g_w = 1.0 / g_sum
    g_idx = jnp.min(jnp.where(g_l == g_max, lane, big), axis=-1, keepdims=True)
    e_lo = N_EGROUPS + g_idx * E_PER_GROUP
    e_l = jnp.where(lane >= e_lo, jnp.where(lane < e_lo + E_PER_GROUP, lg, neg), neg)
    e_max = jnp.max(e_l, axis=-1, keepdims=True)
    e_sum = jnp.sum(jnp.exp(e_l - e_max), axis=-1, keepdims=True)
    i1 = jnp.min(jnp.where(e_l == e_max, lane, big), axis=-1, keepdims=True)
    e_l2 = jnp.where(lane == i1, neg, e_l)
    e_max2 = jnp.max(e_l2, axis=-1, keepdims=True)
    i2 = jnp.min(jnp.where(e_l2 == e_max2, lane, big), axis=-1, keepdims=True)
    p1 = 1.0 / e_sum
    p2 = jnp.exp(e_max2 - e_max) / e_sum
    w1 = g_w * (p1 / (p1 + p2))
    w2 = g_w * (p2 / (p1 + p2))
    id1 = (i1 - N_EGROUPS).astype(F32)
    id2 = (i2 - N_EGROUPS).astype(F32)
    out = jnp.where(lane == 0, id1, jnp.where(lane == 1, id2, jnp.where(lane == 2, w1, jnp.where(lane == 3, w2, 0.0))))
    o_ref[...] = out


def _router(x, w_r, b_r):
    t = x.shape[0]
    tm = ROUTE_TM
    return pl.pallas_call(
        _router_kernel,
        grid=(t // tm,),
        in_specs=[pl.BlockSpec((tm, D_MODEL), lambda i: (i, 0)),
                  pl.BlockSpec((D_MODEL, ROUTE_W), lambda i: (0, 0)),
                  pl.BlockSpec((1, ROUTE_W), lambda i: (0, 0))],
        out_specs=pl.BlockSpec((tm, ROUTE_W), lambda i: (i, 0)),
        out_shape=jax.ShapeDtypeStruct((t, ROUTE_W), F32),
        compiler_params=_cparams("parallel"),
        name="router",
    )(x, w_r, b_r)


def _rows_copy(src, dst, sem, n_rows):
    return pltpu.make_async_copy(src.at[pl.ds(0, n_rows), :], dst.at[pl.ds(0, n_rows), :], sem)


def _expert_kernel(be_ref, asg_ref, nv_ref, x_hbm, wg_ref, wu_ref, wd_ref, y_hbm,
                   xbuf, obuf, wg_b, wu_b, wd_b, gsem, ssem):
    b = pl.program_id(0)
    nb = pl.num_programs(0)
    slot = b % 2
    blk = EXPERT_BLK

    def gather(block, s):
        def body(r, carry):
            tok = lax.shift_right_logical(asg_ref[block * blk + r], 1)
            pltpu.make_async_copy(x_hbm.at[pl.ds(tok, 1), :], xbuf.at[s, pl.ds(r, 1), :], gsem.at[s]).start()
            return carry
        lax.fori_loop(0, blk, body, 0, unroll=GATHER_UNROLL)

    @pl.when(b == 0)
    def _():
        gather(0, 0)

    @pl.when(b + 1 < nb)
    def _():
        gather(b + 1, 1 - slot)

    @pl.when((b == 0) | (be_ref[b] != be_ref[jnp.maximum(b - 1, 0)]))
    def _():
        wg_b[...] = wg_ref[...].astype(BF16)
        wu_b[...] = wu_ref[...].astype(BF16)
        wd_b[...] = wd_ref[...].astype(BF16)

    _rows_copy(x_hbm, xbuf.at[slot], gsem.at[slot], blk).wait()

    def drain(s, n):
        n_tiled = pl.multiple_of(n // SUBLANE * SUBLANE, SUBLANE)

        @pl.when(n_tiled > 0)
        def _():
            _rows_copy(obuf.at[s], y_hbm, ssem.at[s], n_tiled).wait()

        def body(r, carry):
            _rows_copy(obuf.at[s], y_hbm, ssem.at[s], 1).wait()
            return carry
        lax.fori_loop(0, n - n_tiled, body, 0)

    @pl.when(b >= 2)
    def _():
        drain(slot, nv_ref[jnp.maximum(b - 2, 0)])

    n_valid = nv_ref[b]

    @pl.when(n_valid > 0)
    def _():
        x = xbuf[slot].astype(BF16)
        h = _silu(jnp.dot(x, wg_b[...], preferred_element_type=F32)) * \
            jnp.dot(x, wu_b[...], preferred_element_type=F32)
        obuf[slot] = jnp.dot(h.astype(BF16), wd_b[...], preferred_element_type=F32)

        def send(r):
            a = asg_ref[b * blk + r]
            pltpu.make_async_copy(obuf.at[slot, pl.ds(r, 1), :], y_hbm.at[pl.ds(a, 1), :], ssem.at[slot]).start()

        def body_unrolled(i, carry):
            for u in range(GATHER_UNROLL):
                send(i * GATHER_UNROLL + u)
            return carry

        def body(r, carry):
            send(r)
            return carry
        n_groups = n_valid // GATHER_UNROLL
        lax.fori_loop(0, n_groups, body_unrolled, 0)
        lax.fori_loop(n_groups * GATHER_UNROLL, n_valid, body, 0)

    @pl.when(b == nb - 1)
    def _():
        drain(slot, n_valid)

        @pl.when(nb >= 2)
        def _():
            drain(1 - slot, nv_ref[jnp.maximum(b - 1, 0)])


def _experts(x, w_gate, w_up, w_down, layer, block_expert, slot_asg, n_valid):
    nb = block_expert.shape[0]
    t = x.shape[0]
    wspec = lambda r, c: pl.BlockSpec((None, None, r, c), lambda b, be, sa, nv: (layer, be[b], 0, 0))
    grid_spec = pltpu.PrefetchScalarGridSpec(
        num_scalar_prefetch=3,
        grid=(nb,),
        in_specs=[
            pl.BlockSpec(memory_space=pl.ANY),
            wspec(D_MODEL, D_EXPERT), wspec(D_MODEL, D_EXPERT), wspec(D_EXPERT, D_MODEL),
        ],
        out_specs=pl.BlockSpec(memory_space=pl.ANY),
        scratch_shapes=[pltpu.VMEM((2, EXPERT_BLK, D_MODEL), F32),
                        pltpu.VMEM((2, EXPERT_BLK, D_MODEL), F32),
                        pltpu.VMEM((D_MODEL, D_EXPERT), BF16),
                        pltpu.VMEM((D_MODEL, D_EXPERT), BF16),
                        pltpu.VMEM((D_EXPERT, D_MODEL), BF16),
                        pltpu.SemaphoreType.DMA((2,)),
                        pltpu.SemaphoreType.DMA((2,))],
    )
    return pl.pallas_call(
        _expert_kernel,
        grid_spec=grid_spec,
        out_shape=jax.ShapeDtypeStruct((TOP_K * t, D_MODEL), F32),
        compiler_params=_cparams("arbitrary", disable_bounds_checks=True),
        name="experts",
    )(block_expert, slot_asg, n_valid, x, w_gate, w_up, w_down)


def _combine_ln_kernel(y_ref, route_ref, x_ref, g_ref, b_ref, o_ref, ob_ref):
    route = route_ref[...]
    ffn = y_ref[:, :D_MODEL] * route[:, 2:3] + y_ref[:, D_MODEL:] * route[:, 3:4]
    y = _layer_norm(ALPHA * x_ref[...] + ffn, g_ref[...], b_ref[...])
    o_ref[...] = y
    ob_ref[...] = y.astype(BF16)


def _combine_ln(y_asg, route, x, g, b):
    t = x.shape[0]
    tm = LN_TM
    rowspec = pl.BlockSpec((tm, D_MODEL), lambda i: (i, 0))
    const = pl.BlockSpec((1, D_MODEL), lambda i: (0, 0))
    return pl.pallas_call(
        _combine_ln_kernel,
        grid=(t // tm,),
        in_specs=[pl.BlockSpec((tm, TOP_K * D_MODEL), lambda i: (i, 0)),
                  pl.BlockSpec((tm, ROUTE_W), lambda i: (i, 0)),
                  rowspec, const, const],
        out_specs=[rowspec, rowspec],
        out_shape=[jax.ShapeDtypeStruct((t, D_MODEL), F32), jax.ShapeDtypeStruct((t, D_MODEL), BF16)],
        compiler_params=_cparams("parallel"),
        name="combine_ln2",
    )(y_asg.reshape(t, TOP_K * D_MODEL), route, x, g.reshape(1, -1), b.reshape(1, -1))


def _routing_tables(route, t):
    a_n = t * TOP_K
    e_flat = route[:, 0:TOP_K].astype(jnp.int32).reshape(a_n)
    order = jnp.argsort(e_flat).astype(jnp.int32)
    counts = jnp.sum((e_flat[:, None] == jnp.arange(N_EXPERTS, dtype=jnp.int32)[None, :]).astype(jnp.int32), axis=0)
    padded = (counts + EXPERT_BLK - 1) // EXPERT_BLK * EXPERT_BLK
    starts = jnp.cumsum(counts) - counts
    pends = jnp.cumsum(padded)
    pstarts = pends - padded
    n_blocks = a_n // EXPERT_BLK + N_EXPERTS
    block_start = jnp.arange(n_blocks, dtype=jnp.int32) * EXPERT_BLK
    block_expert = jnp.minimum(jnp.sum((block_start[:, None] >= pends[None, :]).astype(jnp.int32), axis=1),
                               N_EXPERTS - 1).astype(jnp.int32)
    in_seg = block_start - pstarts[block_expert]
    n_valid = jnp.clip(counts[block_expert] - in_seg, 0, EXPERT_BLK).astype(jnp.int32)
    r = jnp.arange(EXPERT_BLK, dtype=jnp.int32)[None, :]
    src = (starts[block_expert] + in_seg)[:, None] + r
    valid = r < n_valid[:, None]
    slot_asg = jnp.where(valid, order[jnp.clip(src, 0, a_n - 1)], 0).reshape(-1).astype(jnp.int32)
    return slot_asg, block_expert, n_valid


def _seq_flags(groups, tile):
    first, last = [], []
    for batch, length in groups:
        n = length // tile
        for _ in range(batch):
            first += [1] + [0] * (n - 1)
            last += [0] * (n - 1) + [1]
    return np.asarray(first, np.int32), np.asarray(last, np.int32)


def _trunk(x, xb, groups, w_in, conv_w, conv_b, dt_bias, a_log, d_skip, ssd_norm_w, sgu_norm_w, sgu_norm_b,
           sgu_w, sgu_b, w_branch, w_out, ln1_g, ln1_b, ln2_g, ln2_b,
           w_router_group, b_router_group, w_router_expert, b_router_expert, w_gate, w_up, w_down):
    t = x.shape[0]
    depth = w_in.shape[0]
    conv_first, conv_last = _seq_flags(groups, CONV_TM)
    chunk_first, chunk_last = _seq_flags(groups, CHUNK)
    reset_f = jnp.asarray(chunk_first)
    reset_b = jnp.asarray(chunk_last[::-1].copy())
    conv_first, conv_last = jnp.asarray(conv_first), jnp.asarray(conv_last)
    pad_r = ROUTE_W - N_EGROUPS - N_EXPERTS
    for i in range(depth):
        z = _proj(xb, w_in, i, 0, OFF_XBC, MM_TM, MM_TN, "proj_z")
        xbc = _proj(xb, w_in, i, OFF_XBC, CONV_DIM, MM_TM, MM_TN, "proj_xbc")
        dt_raw = _proj(xb, w_in, i, OFF_DT, 2 * N_HEADS, MM_TM, LANE, "proj_dt")
        uv = _proj(xb, w_in, i, OFF_UV, 2 * D_GMLP, MM_TM, MM_TN, "proj_uv")
        gates = _proj(xb, w_in, i, OFF_GATE, 2 * D_MODEL, MM_TM, MM_TN, "proj_gates")

        xbc = _conv_silu(xbc, conv_w[i], conv_b[i], conv_first, conv_last)
        pre = _ssd_pre(dt_raw, dt_bias[i].reshape(1, 2 * N_HEADS), a_log[i].reshape(1, 2 * N_HEADS))
        y_f = _ssd(xbc, pre, reset_f, rev=False)
        dskip = jnp.repeat(d_skip[i], HEAD_DIM).reshape(1, D_INNER)
        y_ssd = _ssd(xbc, pre, reset_b, rev=True,
                     extra=(y_f, z, dskip, ssd_norm_w[i].reshape(1, D_INNER)))

        sgu_b_full = jnp.repeat(sgu_b[i].T, GMLP_GW, axis=1)
        y_gmlp = _gmlp(uv, sgu_norm_w[i], sgu_norm_b[i], sgu_w[i].astype(BF16), sgu_b_full)

        merged = _merge(y_ssd, y_gmlp, w_branch[i].astype(BF16), gates)
        x, xb = _outproj_ln(merged, w_out[i].astype(BF16), x, ln1_g[i], ln1_b[i])

        w_r = jnp.concatenate([w_router_group[i], w_router_expert[i],
                               jnp.zeros((D_MODEL, pad_r), F32)], axis=1)
        b_r = jnp.concatenate([b_router_group[i], b_router_expert[i], jnp.zeros((pad_r,), F32)]).reshape(1, ROUTE_W)
        route = _router(x, w_r, b_r)
        slot_asg, block_expert, n_valid = _routing_tables(route, t)
        y_asg = _experts(x, w_gate, w_up, w_down, i, block_expert, slot_asg, n_valid)
        x, xb = _combine_ln(y_asg, route, x, ln2_g[i], ln2_b[i])
    return x


def kernel(x_prompt, x_sample, w_in, conv_w, conv_b, dt_bias, a_log, d_skip, ssd_norm_w, sgu_norm_w, sgu_norm_b, sgu_w, sgu_b, w_branch, w_out, ln1_g, ln1_b, ln2_g, ln2_b, w_router_group, b_router_group, w_router_expert, b_router_expert, w_gate, w_up, w_down):
    groups = (x_prompt.shape[:2], x_sample.shape[:2])
    for _, length in groups:
        assert length % CONV_TM == 0 and length % CHUNK == 0
    x = jnp.concatenate([x_prompt.reshape(-1, D_MODEL), x_sample.reshape(-1, D_MODEL)], axis=0)
    t = x.shape[0]
    assert t % MM_TM == 0 and (t * TOP_K) % EXPERT_BLK == 0 and t % (PRE_CHUNKS * CHUNK) == 0
    y = _trunk(x, x.astype(BF16), groups, w_in, conv_w, conv_b, dt_bias, a_log, d_skip, ssd_norm_w,
               sgu_norm_w, sgu_norm_b, sgu_w, sgu_b, w_branch, w_out, ln1_g, ln1_b, ln2_g, ln2_b,
               w_router_group, b_router_group, w_router_expert, b_router_expert, w_gate, w_up, w_down)
    n_p = x_prompt.shape[0] * x_prompt.shape[1]
    return (y[:n_p].reshape(x_prompt.shape), y[n_p:].reshape(x_sample.shape))
```

```python
import functools
import math

import numpy as np
import jax
import jax.numpy as jnp
from jax import lax
from jax.experimental import pallas as pl
from jax.experimental.pallas import tpu as pltpu

F32 = jnp.float32
BF16 = jnp.bfloat16

D_MODEL = 2048
DEPTH = 4
D_INNER = 2 * D_MODEL
HEAD_DIM = 64
N_HEADS = D_INNER // HEAD_DIM
N_GROUPS = 8
GROUP_W = D_INNER // N_GROUPS
D_STATE = 128
D_CONV = 5
CONV_PAD = D_CONV // 2
BC_W = N_GROUPS * D_STATE
CONV_DIM = D_INNER + 2 * BC_W
CHUNK = 128
D_GMLP = D_MODEL
GMLP_GROUPS = 8
GMLP_GW = D_GMLP // GMLP_GROUPS
OFF_XBC = D_INNER
OFF_DT = OFF_XBC + CONV_DIM
OFF_UV = OFF_DT + 2 * N_HEADS
OFF_GATE = OFF_UV + 2 * D_GMLP
N_IN = OFF_GATE + 2 * D_MODEL
N_EGROUPS = 8
E_PER_GROUP = 8
N_EXPERTS = N_EGROUPS * E_PER_GROUP
TOP_K = 2
D_EXPERT = D_MODEL // 4
ALPHA = (2.0 * DEPTH) ** 0.25
EPS = 1e-5

LANE = 128
SUBLANE = 8
V7X_VMEM_BYTES = 64 * 1024 * 1024
VMEM_LIMIT = V7X_VMEM_BYTES * 3 // 4

MM_TM = 1024
MM_TN = 1024
CONV_TM = 512
CONV_TC = 512
PRE_CHUNKS = 8
MERGE_TM = 512
MERGE_TN = 512
LN_TM = 256
ROUTE_TM = 512
EXPERT_BLK = 256
ROUTE_W = LANE


def _cparams(*sem, **kw):
    return pltpu.CompilerParams(dimension_semantics=sem, vmem_limit_bytes=VMEM_LIMIT, **kw)


def _silu(x):
    return x * jax.nn.sigmoid(x)


def _softplus(x):
    return jnp.maximum(x, 0.0) + jnp.log1p(jnp.exp(-jnp.abs(x)))


def _gelu_tanh(x):
    c = math.sqrt(2.0 / math.pi)
    return x * (0.5 * (1.0 + jnp.tanh(c * (x + 0.044715 * (x * x * x)))))


def _layer_norm(v, g, b):
    mu = jnp.mean(v, axis=-1, keepdims=True)
    xc = v - mu
    var = jnp.mean(xc * xc, axis=-1, keepdims=True)
    return xc * lax.rsqrt(var + EPS) * g + b


def _proj_kernel(x_ref, w_hbm, o_ref, w_f32, w_bf, sem, *, layer, col_off, tn):
    @pl.when(pl.program_id(1) == 0)
    def _():
        col = pl.multiple_of(col_off + pl.program_id(0) * tn, LANE)
        copy = pltpu.make_async_copy(w_hbm.at[layer, :, pl.ds(col, tn)], w_f32, sem)
        copy.start()
        copy.wait()
        w_bf[...] = w_f32[...].astype(BF16)

    o_ref[...] = jnp.dot(x_ref[...], w_bf[...], preferred_element_type=F32)


def _proj(x, w_in, layer, col_off, n, tm, tn, name):
    m, k = x.shape
    return pl.pallas_call(
        functools.partial(_proj_kernel, layer=layer, col_off=col_off, tn=tn),
        grid=(n // tn, m // tm),
        in_specs=[pl.BlockSpec((tm, k), lambda j, i: (i, 0)),
                  pl.BlockSpec(memory_space=pl.ANY)],
        out_specs=pl.BlockSpec((tm, tn), lambda j, i: (i, j)),
        out_shape=jax.ShapeDtypeStruct((m, n), F32),
        scratch_shapes=[pltpu.VMEM((k, tn), F32), pltpu.VMEM((k, tn), BF16), pltpu.SemaphoreType.DMA],
        compiler_params=_cparams("arbitrary", "arbitrary"),
        name=name,
    )(x, w_in)


def _conv_kernel(first_ref, last_ref, xp_ref, x_ref, xn_ref, w_ref, b_ref, o_ref, ext_ref, *, tm):
    i = pl.program_id(0)
    keep_prev = jnp.where(first_ref[i] == 1, 0.0, 1.0).astype(F32)
    keep_next = jnp.where(last_ref[i] == 1, 0.0, 1.0).astype(F32)
    ext_ref[0:SUBLANE, :] = xp_ref[...] * keep_prev
    ext_ref[SUBLANE:SUBLANE + tm, :] = x_ref[...]
    ext_ref[SUBLANE + tm:2 * SUBLANE + tm, :] = xn_ref[...] * keep_next
    ext = ext_ref[...]
    n_ext = tm + 2 * SUBLANE
    acc = jnp.broadcast_to(b_ref[...], (tm, b_ref.shape[1]))
    for k in range(D_CONV):
        shift = (CONV_PAD - k) % n_ext
        tap = ext if shift == 0 else pltpu.roll(ext, shift, 0)
        acc = acc + w_ref[k:k + 1, :] * tap[SUBLANE:SUBLANE + tm, :]
    o_ref[...] = _silu(acc)


def _conv_silu(xbc, conv_w, conv_b, first, last):
    t, c = xbc.shape
    tm, tc = CONV_TM, CONV_TC
    rb = tm // SUBLANE
    last_rb = t // SUBLANE - 1
    grid_spec = pltpu.PrefetchScalarGridSpec(
        num_scalar_prefetch=2,
        grid=(t // tm, c // tc),
        in_specs=[
            pl.BlockSpec((SUBLANE, tc), lambda i, j, f, l: (jnp.maximum(i * rb - 1, 0), j)),
            pl.BlockSpec((tm, tc), lambda i, j, f, l: (i, j)),
            pl.BlockSpec((SUBLANE, tc), lambda i, j, f, l: (jnp.minimum((i + 1) * rb, last_rb), j)),
            pl.BlockSpec((D_CONV, tc), lambda i, j, f, l: (0, j)),
            pl.BlockSpec((1, tc), lambda i, j, f, l: (0, j)),
        ],
        out_specs=pl.BlockSpec((tm, tc), lambda i, j, f, l: (i, j)),
        scratch_shapes=[pltpu.VMEM((tm + 2 * SUBLANE, tc), F32)],
    )
    return pl.pallas_call(
        functools.partial(_conv_kernel, tm=tm),
        grid_spec=grid_spec,
        out_shape=jax.ShapeDtypeStruct((t, c), F32),
        compiler_params=_cparams("parallel", "parallel"),
        name="conv_silu",
    )(first, last, xbc, xbc, xbc, conv_w, conv_b.reshape(1, c))


def _ssd_pre_kernel(dt_ref, dtb_ref, alog_ref, dtt_ref, p_ref, pt_ref, w_ref, rows_ref):
    c = CHUNK
    row = lax.broadcasted_iota(jnp.int32, (c, c), 0)
    col = lax.broadcasted_iota(jnp.int32, (c, c), 1)
    tri = (col <= row).astype(F32)
    fwd = col < N_HEADS
    neg_a_rate = -jnp.exp(alog_ref[...])
    for k in range(PRE_CHUNKS):
        rs = slice(k * c, (k + 1) * c)
        dt = _softplus(dt_ref[rs, :] + dtb_ref[...])
        a = dt * neg_a_rate
        cum = jnp.dot(tri, a, precision=lax.Precision.HIGHEST, preferred_element_type=F32)
        total = cum[c - 1:c, :]
        p = jnp.where(fwd, cum, a - cum)
        dte = jnp.exp(jnp.where(fwd, total - p, -p))
        dtt_ref[rs, :] = dt.T
        p_ref[rs, :] = p
        pt_ref[rs, :] = p.T
        w_ref[rs, :] = dt * dte
        sub = lax.broadcasted_iota(jnp.int32, (SUBLANE, c), 0)
        off = jnp.where(fwd[:1], 0.0, total)
        rows_ref[k * SUBLANE:(k + 1) * SUBLANE, :] = jnp.where(
            sub == 0, off, jnp.where(sub == 1, jnp.exp(total), 0.0))


def _ssd_pre(dt_raw, dt_bias, a_log):
    t = dt_raw.shape[0]
    rows = PRE_CHUNKS * CHUNK
    blk = pl.BlockSpec((rows, LANE), lambda i: (i, 0))
    const = pl.BlockSpec((1, LANE), lambda i: (0, 0))
    tok = jax.ShapeDtypeStruct((t, LANE), F32)
    return pl.pallas_call(
        _ssd_pre_kernel,
        grid=(t // rows,),
        in_specs=[blk, const, const],
        out_specs=[blk, blk, blk, blk, pl.BlockSpec((PRE_CHUNKS * SUBLANE, LANE), lambda i: (i, 0))],
        out_shape=[tok, tok, tok, tok, jax.ShapeDtypeStruct((t // CHUNK * SUBLANE, LANE), F32)],
        compiler_params=_cparams("parallel"),
        name="ssd_pre",
    )(dt_raw, dt_bias, a_log)


def _ssd_kernel(reset_ref, xs_ref, bm_ref, cm_ref, dtt_ref, p_ref, pt_ref, w_ref, rows_ref, *rest, rev):
    if rev:
        yf_ref, z_ref, dskip_ref, nw_ref, o_ref, h_ref, xdec_ref, y_ref = rest
    else:
        o_ref, h_ref, xdec_ref = rest
    c = CHUNK
    step = pl.program_id(0)

    @pl.when(reset_ref[step] == 1)
    def _():
        h_ref[...] = jnp.zeros_like(h_ref)

    row = lax.broadcasted_iota(jnp.int32, (c, c), 0)
    col = lax.broadcasted_iota(jnp.int32, (c, c), 1)
    mask = (row <= col) if rev else (row >= col)
    left = col < HEAD_DIM
    lo = N_HEADS if rev else 0
    p = p_ref[...]
    w = w_ref[...]
    rows = rows_ref[...]
    off = rows[0:1, :]
    cdec = rows[1:2, :]

    def expand_pair(v, h0, n_rows):
        return jnp.where(left[:n_rows], jnp.broadcast_to(v[:, h0:h0 + 1], (n_rows, LANE)),
                         jnp.broadcast_to(v[:, h0 + 1:h0 + 2], (n_rows, LANE)))

    pairs = GROUP_W // LANE
    for g in range(N_GROUPS):
        for jj in range(pairs):
            j = g * pairs + jj
            sl = slice(j * LANE, (j + 1) * LANE)
            xdec_ref[:, sl] = (xs_ref[:, sl] * expand_pair(w, lo + 2 * j, c)).astype(BF16)
        gs = slice(g * D_STATE, (g + 1) * D_STATE)
        ws = slice(g * GROUP_W, (g + 1) * GROUP_W)
        bm_g = bm_ref[:, gs]
        cm_g = cm_ref[:, gs]
        cb = lax.dot_general(cm_g.astype(BF16), bm_g.astype(BF16), (((1,), (1,)), ((), ())),
                             preferred_element_type=F32)
        h_prev = h_ref[g]
        for jj in range(pairs):
            h0 = lo + g * 2 * pairs + 2 * jj
            sl = slice(g * GROUP_W + jj * LANE, g * GROUP_W + (jj + 1) * LANE)
            rhs = jnp.concatenate([xs_ref[:, sl].astype(BF16),
                                   h_prev[:, jj * LANE:(jj + 1) * LANE].astype(BF16)], axis=0)
            halves = []
            for h in (h0, h0 + 1):
                p_col = jnp.broadcast_to(p[:, h:h + 1], (c, c))
                decay = jnp.exp(jnp.where(mask, p_col - pt_ref[h:h + 1, :], -jnp.inf))
                scores = cb * decay * dtt_ref[h:h + 1, :]
                from_state = cm_g * jnp.exp(p_col + off[:, h:h + 1])
                lhs = jnp.concatenate([scores.astype(BF16), from_state.astype(BF16)], axis=1)
                halves.append(jnp.dot(lhs, rhs, preferred_element_type=F32))
            y_t = jnp.where(left, halves[0], halves[1])
            if rev:
                y_t = yf_ref[:, sl] + y_t + xs_ref[:, sl] * dskip_ref[:, sl]
                y_ref[:, sl] = y_t * _silu(z_ref[:, sl])
            else:
                o_ref[:, sl] = y_t
        chunk_state = jnp.dot(bm_g.T.astype(BF16), xdec_ref[:, ws], preferred_element_type=F32)
        cd_row = jnp.concatenate([expand_pair(cdec, lo + g * 2 * pairs + 2 * jj, 1)
                                  for jj in range(pairs)], axis=1)
        h_ref[g] = h_prev * cd_row + chunk_state
        if rev:
            yg = y_ref[:, ws]
            ms = jnp.mean(yg * yg, axis=-1, keepdims=True)
            o_ref[:, ws] = (yg * lax.rsqrt(ms + EPS) * nw_ref[:, ws]).astype(o_ref.dtype)


def _ssd(xbc, pre, reset, *, rev, extra=()):
    t = xbc.shape[0]
    nc = t // CHUNK
    if rev:
        cidx = lambda s: nc - 1 - s
    else:
        cidx = lambda s: s
    bm_blk = D_INNER // BC_W
    row = lambda w: pl.BlockSpec((1, w), lambda s, r: (0, 0))
    tok = pl.BlockSpec((CHUNK, LANE), lambda s, r: (cidx(s), 0))
    wide = pl.BlockSpec((CHUNK, D_INNER), lambda s, r: (cidx(s), 0))
    in_specs = [
        wide,
        pl.BlockSpec((CHUNK, BC_W), lambda s, r: (cidx(s), bm_blk)),
        pl.BlockSpec((CHUNK, BC_W), lambda s, r: (cidx(s), bm_blk + 1)),
        tok, tok, tok, tok,
        pl.BlockSpec((SUBLANE, LANE), lambda s, r: (cidx(s), 0)),
    ]
    scratch = [pltpu.VMEM((N_GROUPS, D_STATE, GROUP_W), F32),
               pltpu.VMEM((CHUNK, D_INNER), BF16)]
    args = [reset, xbc, xbc, xbc, *pre]
    if rev:
        in_specs += [wide, wide, row(D_INNER), row(D_INNER)]
        scratch.append(pltpu.VMEM((CHUNK, D_INNER), F32))
        args += list(extra)
        out_dtype = BF16
    else:
        out_dtype = F32
    grid_spec = pltpu.PrefetchScalarGridSpec(
        num_scalar_prefetch=1,
        grid=(nc,),
        in_specs=in_specs,
        out_specs=wide,
        scratch_shapes=scratch,
    )
    return pl.pallas_call(
        functools.partial(_ssd_kernel, rev=rev),
        grid_spec=grid_spec,
        out_shape=jax.ShapeDtypeStruct((t, D_INNER), out_dtype),
        compiler_params=_cparams("arbitrary"),
        name="ssd_bwd" if rev else "ssd_fwd",
    )(*args)


def _gmlp_kernel(uv_ref, nw_ref, nb_ref, sw_ref, sb_ref, o_ref):
    v = _layer_norm(_gelu_tanh(uv_ref[:, D_GMLP:]), nw_ref[...], nb_ref[...])
    for g in range(GMLP_GROUPS):
        sl = slice(g * GMLP_GW, (g + 1) * GMLP_GW)
        v_mix = jnp.dot(sw_ref[g], v[:, sl].astype(BF16), preferred_element_type=F32) + sb_ref[:, sl]
        o_ref[:, sl] = (_gelu_tanh(uv_ref[:, sl]) * v_mix).astype(o_ref.dtype)


def _gmlp(uv, norm_w, norm_b, sgu_w, sgu_b_full):
    t = uv.shape[0]
    const = lambda shape: pl.BlockSpec(shape, lambda i: (0,) * len(shape))
    return pl.pallas_call(
        _gmlp_kernel,
        grid=(t // CHUNK,),
        in_specs=[pl.BlockSpec((CHUNK, 2 * D_GMLP), lambda i: (i, 0)),
                  const((1, D_GMLP)), const((1, D_GMLP)),
                  const((GMLP_GROUPS, CHUNK, CHUNK)), const((CHUNK, D_GMLP))],
        out_specs=pl.BlockSpec((CHUNK, D_GMLP), lambda i: (i, 0)),
        out_shape=jax.ShapeDtypeStruct((t, D_GMLP), BF16),
        compiler_params=_cparams("parallel"),
        name="gmlp",
    )(uv, norm_w.reshape(1, -1), norm_b.reshape(1, -1), sgu_w, sgu_b_full)


def _merge_kernel(ys_ref, yg_ref, wa_ref, wb_ref, ga_ref, gb_ref, o_ref):
    p_a = jnp.dot(ys_ref[...], wa_ref[...], preferred_element_type=F32)
    p_b = jnp.dot(yg_ref[...], wb_ref[...], preferred_element_type=F32)
    merged = jax.nn.sigmoid(ga_ref[...]) * p_a + jax.nn.sigmoid(gb_ref[...]) * p_b
    o_ref[...] = merged.astype(o_ref.dtype)


def _merge(y_ssd, y_gmlp, w_branch, gates):
    t = y_ssd.shape[0]
    tm, tn = MERGE_TM, MERGE_TN
    gb_off = D_MODEL // tn
    return pl.pallas_call(
        _merge_kernel,
        grid=(t // tm, D_MODEL // tn),
        in_specs=[pl.BlockSpec((tm, D_INNER), lambda i, j: (i, 0)),
                  pl.BlockSpec((tm, D_GMLP), lambda i, j: (i, 0)),
                  pl.BlockSpec((D_INNER, tn), lambda i, j: (0, j)),
                  pl.BlockSpec((D_GMLP, tn), lambda i, j: (D_INNER // D_GMLP, j)),
                  pl.BlockSpec((tm, tn), lambda i, j: (i, j)),
                  pl.BlockSpec((tm, tn), lambda i, j: (i, j + gb_off))],
        out_specs=pl.BlockSpec((tm, tn), lambda i, j: (i, j)),
        out_shape=jax.ShapeDtypeStruct((t, D_MODEL), BF16),
        compiler_params=_cparams("parallel", "parallel"),
        name="branch_merge",
    )(y_ssd, y_gmlp, w_branch, w_branch, gates, gates)


def _outproj_ln_kernel(m_ref, w_ref, x_ref, g_ref, b_ref, o_ref, ob_ref):
    mix = jnp.dot(m_ref[...], w_ref[...], preferred_element_type=F32)
    y = _layer_norm(ALPHA * x_ref[...] + mix, g_ref[...], b_ref[...])
    o_ref[...] = y
    ob_ref[...] = y.astype(BF16)


def _outproj_ln(merged, w_out, x, g, b):
    t = x.shape[0]
    tm = LN_TM
    rowspec = pl.BlockSpec((tm, D_MODEL), lambda i: (i, 0))
    const = lambda shape: pl.BlockSpec(shape, lambda i: (0, 0))
    return pl.pallas_call(
        _outproj_ln_kernel,
        grid=(t // tm,),
        in_specs=[rowspec, const((D_MODEL, D_MODEL)), rowspec, const((1, D_MODEL)), const((1, D_MODEL))],
        out_specs=[rowspec, rowspec],
        out_shape=[jax.ShapeDtypeStruct((t, D_MODEL), F32), jax.ShapeDtypeStruct((t, D_MODEL), BF16)],
        compiler_params=_cparams("parallel"),
        name="outproj_ln1",
    )(merged, w_out, x, g.reshape(1, -1), b.reshape(1, -1))


def _router_kernel(x_ref, w_ref, b_ref, o_ref):
    lg = jnp.dot(x_ref[...], w_ref[...], precision=lax.Precision.HIGHEST,
                 preferred_element_type=F32) + b_ref[...]
    lane = lax.broadcasted_iota(jnp.int32, lg.shape, 1)
    big = jnp.int32(ROUTE_W)
    neg = -jnp.inf
    g_l = jnp.where(lane < N_EGROUPS, lg, neg)
    g_max = jnp.max(g_l, axis=-1, keepdims=True)
    g_sum = jnp.sum(jnp.exp(g_l - g_max), axis=-1, keepdims=True)
    g_w = 1.0 / g_sum
    g_idx = jnp.min(jnp.where(g_l == g_max, lane, big), axis=-1, keepdims=True)
    e_lo = N_EGROUPS + g_idx * E_PER_GROUP
    e_l = jnp.where(lane >= e_lo, jnp.where(lane < e_lo + E_PER_GROUP, lg, neg), neg)
    e_max = jnp.max(e_l, axis=-1, keepdims=True)
    e_sum = jnp.sum(jnp.exp(e_l - e_max), axis=-1, keepdims=True)
    i1 = jnp.min(jnp.where(e_l == e_max, lane, big), axis=-1, keepdims=True)
    e_l2 = jnp.where(lane == i1, neg, e_l)
    e_max2 = jnp.max(e_l2, axis=-1, keepdims=True)
    i2 = jnp.min(jnp.where(e_l2 == e_max2, lane, big), axis=-1, keepdims=True)
    p1 = 1.0 / e_sum
    p2 = jnp.exp(e_max2 - e_max) / e_sum
    w1 = g_w * (p1 / (p1 + p2))
    w2 = g_w * (p2 / (p1 + p2))
    id1 = (i1 - N_EGROUPS).astype(F32)
    id2 = (i2 - N_EGROUPS).astype(F32)
    out = jnp.where(lane == 0, id1, jnp.where(lane == 1, id2, jnp.where(lane == 2, w1, jnp.where(lane == 3, w2, 0.0))))
    o_ref[...] = out


def _router(x, w_r, b_r):
    t = x.shape[0]
    tm = ROUTE_TM
    return pl.pallas_call(
        _router_kernel,
        grid=(t // tm,),
        in_specs=[pl.BlockSpec((tm, D_MODEL), lambda i: (i, 0)),
                  pl.BlockSpec((D_MODEL, ROUTE_W), lambda i: (0, 0)),
                  pl.BlockSpec((1, ROUTE_W), lambda i: (0, 0))],
        out_specs=pl.BlockSpec((tm, ROUTE_W), lambda i: (i, 0)),
        out_shape=jax.ShapeDtypeStruct((t, ROUTE_W), F32),
        compiler_params=_cparams("parallel"),
        name="router",
    )(x, w_r, b_r)


def _rows_copy(src, dst, sem, n_rows):
    return pltpu.make_async_copy(src.at[pl.ds(0, n_rows), :], dst.at[pl.ds(0, n_rows), :], sem)


def _expert_kernel(be_ref, tok_ref, dst_ref, x_hbm, wg_ref, wu_ref, wd_ref, y_hbm,
                   xbuf0, xbuf1, obuf0, obuf1, wg_b, wu_b, wd_b, gsem, ssem, *, n_blocks, spare_row):
    b = pl.program_id(0)
    blk = EXPERT_BLK
    xbufs = (xbuf0, xbuf1)
    obufs = (obuf0, obuf1)

    def gather(block, s):
        for r in range(blk):
            tok = tok_ref[block * blk + r]
            pltpu.make_async_copy(x_hbm.at[pl.ds(tok, 1), :], xbufs[s].at[pl.ds(r, 1), :], gsem.at[s]).start()

    def scatter(block, s):
        for r in range(blk):
            row = dst_ref[block * blk + r]
            pltpu.make_async_copy(obufs[s].at[pl.ds(r, 1), :], y_hbm.at[pl.ds(row, 1), :], ssem.at[s]).start()

    @pl.when(b == 0)
    def _():
        gather(0, 0)
        for s in range(2):
            obufs[s][...] = jnp.zeros_like(obufs[s])
            pltpu.make_async_copy(obufs[s], y_hbm.at[pl.ds(spare_row + s * blk, blk), :], ssem.at[s]).start()

    @pl.when((b == 0) | (be_ref[b] != be_ref[jnp.maximum(b - 1, 0)]))
    def _():
        wg_b[...] = wg_ref[...].astype(BF16)
        wu_b[...] = wu_ref[...].astype(BF16)
        wd_b[...] = wd_ref[...].astype(BF16)

    def step(s):
        _rows_copy(x_hbm, xbufs[s], gsem.at[s], blk).wait()
        _rows_copy(obufs[s], y_hbm, ssem.at[s], blk).wait()
        gather(jnp.minimum(b + 1, n_blocks - 1), 1 - s)
        x = xbufs[s][...].astype(BF16)
        h = _silu(jnp.dot(x, wg_b[...], preferred_element_type=F32)) * \
            jnp.dot(x, wu_b[...], preferred_element_type=F32)
        obufs[s][...] = jnp.dot(h.astype(BF16), wd_b[...], preferred_element_type=F32)
        scatter(b, s)

    for s in range(2):
        pl.when(b % 2 == s)(functools.partial(step, s))

    @pl.when(b == n_blocks - 1)
    def _():
        s_last = (n_blocks - 1) % 2
        _rows_copy(x_hbm, xbufs[1 - s_last], gsem.at[1 - s_last], blk).wait()
        for s in range(2):
            _rows_copy(obufs[s], y_hbm, ssem.at[s], blk).wait()


def _experts(x, w_gate, w_up, w_down, layer, block_expert, slot_tok, slot_dst):
    nb = block_expert.shape[0]
    t = x.shape[0]
    assert nb >= 2
    wspec = lambda r, c: pl.BlockSpec((None, None, r, c), lambda b, be, st, sd: (layer, be[b], 0, 0))
    row_buf = pltpu.VMEM((EXPERT_BLK, D_MODEL), F32)
    grid_spec = pltpu.PrefetchScalarGridSpec(
        num_scalar_prefetch=3,
        grid=(nb,),
        in_specs=[
            pl.BlockSpec(memory_space=pl.ANY),
            wspec(D_MODEL, D_EXPERT), wspec(D_MODEL, D_EXPERT), wspec(D_EXPERT, D_MODEL),
        ],
        out_specs=pl.BlockSpec(memory_space=pl.ANY),
        scratch_shapes=[row_buf, row_buf, row_buf, row_buf,
                        pltpu.VMEM((D_MODEL, D_EXPERT), BF16),
                        pltpu.VMEM((D_MODEL, D_EXPERT), BF16),
                        pltpu.VMEM((D_EXPERT, D_MODEL), BF16),
                        pltpu.SemaphoreType.DMA((2,)),
                        pltpu.SemaphoreType.DMA((2,))],
    )
    return pl.pallas_call(
        functools.partial(_expert_kernel, n_blocks=nb, spare_row=TOP_K * t),
        grid_spec=grid_spec,
        out_shape=jax.ShapeDtypeStruct((TOP_K * t + 2 * EXPERT_BLK, D_MODEL), F32),
        compiler_params=_cparams("arbitrary", disable_bounds_checks=True),
        name="experts",
    )(block_expert, slot_tok, slot_dst, x, w_gate, w_up, w_down)


def _combine_ln_kernel(y1_ref, y2_ref, route_ref, x_ref, g_ref, b_ref, o_ref, ob_ref):
    route = route_ref[...]
    ffn = y1_ref[...] * route[:, 2:3] + y2_ref[...] * route[:, 3:4]
    y = _layer_norm(ALPHA * x_ref[...] + ffn, g_ref[...], b_ref[...])
    o_ref[...] = y
    ob_ref[...] = y.astype(BF16)


def _combine_ln(y_asg, route, x, g, b):
    t = x.shape[0]
    tm = LN_TM
    rowspec = pl.BlockSpec((tm, D_MODEL), lambda i: (i, 0))
    const = pl.BlockSpec((1, D_MODEL), lambda i: (0, 0))
    return pl.pallas_call(
        _combine_ln_kernel,
        grid=(t // tm,),
        in_specs=[rowspec,
                  pl.BlockSpec((tm, D_MODEL), lambda i: (i + t // tm, 0)),
                  pl.BlockSpec((tm, ROUTE_W), lambda i: (i, 0)),
                  rowspec, const, const],
        out_specs=[rowspec, rowspec],
        out_shape=[jax.ShapeDtypeStruct((t, D_MODEL), F32), jax.ShapeDtypeStruct((t, D_MODEL), BF16)],
        compiler_params=_cparams("parallel"),
        name="combine_ln2",
    )(y_asg, y_asg, route, x, g.reshape(1, -1), b.reshape(1, -1))


def _routing_tables(route, t):
    a_n = t * TOP_K
    e_flat = route[:, 0:TOP_K].astype(jnp.int32).T.reshape(a_n)
    order = jnp.argsort(e_flat).astype(jnp.int32)
    counts = jnp.sum((e_flat[:, None] == jnp.arange(N_EXPERTS, dtype=jnp.int32)[None, :]).astype(jnp.int32), axis=0)
    padded = (counts + EXPERT_BLK - 1) // EXPERT_BLK * EXPERT_BLK
    starts = jnp.cumsum(counts) - counts
    pends = jnp.cumsum(padded)
    pstarts = pends - padded
    n_blocks = a_n // EXPERT_BLK + N_EXPERTS
    block = jnp.arange(n_blocks, dtype=jnp.int32)
    block_start = block * EXPERT_BLK
    block_expert = jnp.minimum(jnp.sum((block_start[:, None] >= pends[None, :]).astype(jnp.int32), axis=1),
                               N_EXPERTS - 1).astype(jnp.int32)
    in_seg = block_start - pstarts[block_expert]
    n_valid = jnp.clip(counts[block_expert] - in_seg, 0, EXPERT_BLK)
    r = jnp.arange(EXPERT_BLK, dtype=jnp.int32)[None, :]
    src = (starts[block_expert] + in_seg)[:, None] + r
    valid = r < n_valid[:, None]
    asg = order[jnp.clip(src, 0, a_n - 1)]
    slot_tok = jnp.where(valid, jnp.where(asg >= t, asg - t, asg), 0).reshape(-1).astype(jnp.int32)
    spare = a_n + (block % 2)[:, None] * EXPERT_BLK + r
    slot_dst = jnp.where(valid, asg, spare).reshape(-1).astype(jnp.int32)
    return slot_tok, slot_dst, block_expert


def _seq_flags(groups, tile):
    first, last = [], []
    for batch, length in groups:
        n = length // tile
        for _ in range(batch):
            first += [1] + [0] * (n - 1)
            last += [0] * (n - 1) + [1]
    return np.asarray(first, np.int32), np.asarray(last, np.int32)


def _trunk(x, xb, groups, w_in, conv_w, conv_b, dt_bias, a_log, d_skip, ssd_norm_w, sgu_norm_w, sgu_norm_b,
           sgu_w, sgu_b, w_branch, w_out, ln1_g, ln1_b, ln2_g, ln2_b,
           w_router_group, b_router_group, w_router_expert, b_router_expert, w_gate, w_up, w_down):
    t = x.shape[0]
    depth = w_in.shape[0]
    conv_first, conv_last = _seq_flags(groups, CONV_TM)
    chunk_first, chunk_last = _seq_flags(groups, CHUNK)
    reset_f = jnp.asarray(chunk_first)
    reset_b = jnp.asarray(chunk_last[::-1].copy())
    conv_first, conv_last = jnp.asarray(conv_first), jnp.asarray(conv_last)
    pad_r = ROUTE_W - N_EGROUPS - N_EXPERTS
    for i in range(depth):
        z = _proj(xb, w_in, i, 0, OFF_XBC, MM_TM, MM_TN, "proj_z")
        xbc = _proj(xb, w_in, i, OFF_XBC, CONV_DIM, MM_TM, MM_TN, "proj_xbc")
        dt_raw = _proj(xb, w_in, i, OFF_DT, 2 * N_HEADS, MM_TM, LANE, "proj_dt")
        uv = _proj(xb, w_in, i, OFF_UV, 2 * D_GMLP, MM_TM, MM_TN, "proj_uv")
        gates = _proj(xb, w_in, i, OFF_GATE, 2 * D_MODEL, MM_TM, MM_TN, "proj_gates")

        xbc = _conv_silu(xbc, conv_w[i], conv_b[i], conv_first, conv_last)
        pre = _ssd_pre(dt_raw, dt_bias[i].reshape(1, 2 * N_HEADS), a_log[i].reshape(1, 2 * N_HEADS))
        y_f = _ssd(xbc, pre, reset_f, rev=False)
        dskip = jnp.repeat(d_skip[i], HEAD_DIM).reshape(1, D_INNER)
        y_ssd = _ssd(xbc, pre, reset_b, rev=True,
                     extra=(y_f, z, dskip, ssd_norm_w[i].reshape(1, D_INNER)))

        sgu_b_full = jnp.repeat(sgu_b[i].T, GMLP_GW, axis=1)
        y_gmlp = _gmlp(uv, sgu_norm_w[i], sgu_norm_b[i], sgu_w[i].astype(BF16), sgu_b_full)

        merged = _merge(y_ssd, y_gmlp, w_branch[i].astype(BF16), gates)
        x, xb = _outproj_ln(merged, w_out[i].astype(BF16), x, ln1_g[i], ln1_b[i])

        w_r = jnp.concatenate([w_router_group[i], w_router_expert[i],
                               jnp.zeros((D_MODEL, pad_r), F32)], axis=1)
        b_r = jnp.concatenate([b_router_group[i], b_router_expert[i], jnp.zeros((pad_r,), F32)]).reshape(1, ROUTE_W)
        route = _router(x, w_r, b_r)
        slot_tok, slot_dst, block_expert = _routing_tables(route, t)
        y_asg = _experts(x, w_gate, w_up, w_down, i, block_expert, slot_tok, slot_dst)
        x, xb = _combine_ln(y_asg, route, x, ln2_g[i], ln2_b[i])
    return x


def kernel(x_prompt, x_sample, w_in, conv_w, conv_b, dt_bias, a_log, d_skip, ssd_norm_w, sgu_norm_w, sgu_norm_b, sgu_w, sgu_b, w_branch, w_out, ln1_g, ln1_b, ln2_g, ln2_b, w_router_group, b_router_group, w_router_expert, b_router_expert, w_gate, w_up, w_down):
    groups = (x_prompt.shape[:2], x_sample.shape[:2])
    for _, length in groups:
        assert length % CONV_TM == 0 and length % CHUNK == 0
    x = jnp.concatenate([x_prompt.reshape(-1, D_MODEL), x_sample.reshape(-1, D_MODEL)], axis=0)
    t = x.shape[0]
    assert t % MM_TM == 0 and (t * TOP_K) % EXPERT_BLK == 0 and t % (PRE_CHUNKS * CHUNK) == 0
    y = _trunk(x, x.astype(BF16), groups, w_in, conv_w, conv_b, dt_bias, a_log, d_skip, ssd_norm_w,
               sgu_norm_w, sgu_norm_b, sgu_w, sgu_b, w_branch, w_out, ln1_g, ln1_b, ln2_g, ln2_b,
               w_router_group, b_router_group, w_router_expert, b_router_expert, w_gate, w_up, w_down)
    n_p = x_prompt.shape[0] * x_prompt.shape[1]
    return (y[:n_p].reshape(x_prompt.shape), y[n_p:].reshape(x_sample.shape))
```

```python
import functools
import math

import numpy as np
import jax
import jax.numpy as jnp
from jax import lax
from jax.experimental import pallas as pl
from jax.experimental.pallas import tpu as pltpu

F32 = jnp.float32
BF16 = jnp.bfloat16

D_MODEL = 2048
DEPTH = 4
D_INNER = 2 * D_MODEL
HEAD_DIM = 64
N_HEADS = D_INNER // HEAD_DIM
N_GROUPS = 8
GROUP_W = D_INNER // N_GROUPS
D_STATE = 128
D_CONV = 5
CONV_PAD = D_CONV // 2
BC_W = N_GROUPS * D_STATE
CONV_DIM = D_INNER + 2 * BC_W
CHUNK = 128
D_GMLP = D_MODEL
GMLP_GROUPS = 8
GMLP_GW = D_GMLP // GMLP_GROUPS
OFF_XBC = D_INNER
OFF_DT = OFF_XBC + CONV_DIM
OFF_UV = OFF_DT + 2 * N_HEADS
OFF_GATE = OFF_UV + 2 * D_GMLP
N_IN = OFF_GATE + 2 * D_MODEL
N_EGROUPS = 8
E_PER_GROUP = 8
N_EXPERTS = N_EGROUPS * E_PER_GROUP
TOP_K = 2
D_EXPERT = D_MODEL // 4
ALPHA = (2.0 * DEPTH) ** 0.25
EPS = 1e-5

LANE = 128
SUBLANE = 8
V7X_VMEM_BYTES = 64 * 1024 * 1024
VMEM_LIMIT = V7X_VMEM_BYTES * 3 // 4

MM_TM = 1024
MM_TN = 1024
CONV_TM = 512
CONV_TC = 512
PRE_CHUNKS = 8
MERGE_TM = 512
MERGE_TN = 512
LN_TM = 256
ROUTE_TM = 512
EXPERT_BLK = 256
ROUTE_W = LANE


def _cparams(*sem, **kw):
    return pltpu.CompilerParams(dimension_semantics=sem, vmem_limit_bytes=VMEM_LIMIT, **kw)


def _silu(x):
    return x * jax.nn.sigmoid(x)


def _softplus(x):
    return jnp.maximum(x, 0.0) + jnp.log1p(jnp.exp(-jnp.abs(x)))


def _gelu_tanh(x):
    c = math.sqrt(2.0 / math.pi)
    return x * (0.5 * (1.0 + jnp.tanh(c * (x + 0.044715 * (x * x * x)))))


def _layer_norm(v, g, b):
    mu = jnp.mean(v, axis=-1, keepdims=True)
    xc = v - mu
    var = jnp.mean(xc * xc, axis=-1, keepdims=True)
    return xc * lax.rsqrt(var + EPS) * g + b


def _proj_kernel(x_ref, w_hbm, o_ref, w_f32, w_bf, sem, *, layer, col_off, tn):
    @pl.when(pl.program_id(1) == 0)
    def _():
        col = pl.multiple_of(col_off + pl.program_id(0) * tn, LANE)
        copy = pltpu.make_async_copy(w_hbm.at[layer, :, pl.ds(col, tn)], w_f32, sem)
        copy.start()
        copy.wait()
        w_bf[...] = w_f32[...].astype(BF16)

    o_ref[...] = jnp.dot(x_ref[...], w_bf[...], preferred_element_type=F32)


def _proj(x, w_in, layer, col_off, n, tm, tn, name):
    m, k = x.shape
    return pl.pallas_call(
        functools.partial(_proj_kernel, layer=layer, col_off=col_off, tn=tn),
        grid=(n // tn, m // tm),
        in_specs=[pl.BlockSpec((tm, k), lambda j, i: (i, 0)),
                  pl.BlockSpec(memory_space=pl.ANY)],
        out_specs=pl.BlockSpec((tm, tn), lambda j, i: (i, j)),
        out_shape=jax.ShapeDtypeStruct((m, n), F32),
        scratch_shapes=[pltpu.VMEM((k, tn), F32), pltpu.VMEM((k, tn), BF16), pltpu.SemaphoreType.DMA],
        compiler_params=_cparams("arbitrary", "arbitrary"),
        name=name,
    )(x, w_in)


def _conv_kernel(first_ref, last_ref, xp_ref, x_ref, xn_ref, w_ref, b_ref, o_ref, ext_ref, *, tm):
    i = pl.program_id(0)
    keep_prev = jnp.where(first_ref[i] == 1, 0.0, 1.0).astype(F32)
    keep_next = jnp.where(last_ref[i] == 1, 0.0, 1.0).astype(F32)
    ext_ref[0:SUBLANE, :] = xp_ref[...] * keep_prev
    ext_ref[SUBLANE:SUBLANE + tm, :] = x_ref[...]
    ext_ref[SUBLANE + tm:2 * SUBLANE + tm, :] = xn_ref[...] * keep_next
    ext = ext_ref[...]
    n_ext = tm + 2 * SUBLANE
    acc = jnp.broadcast_to(b_ref[...], (tm, b_ref.shape[1]))
    for k in range(D_CONV):
        shift = (CONV_PAD - k) % n_ext
        tap = ext if shift == 0 else pltpu.roll(ext, shift, 0)
        acc = acc + w_ref[k:k + 1, :] * tap[SUBLANE:SUBLANE + tm, :]
    o_ref[...] = _silu(acc)


def _conv_silu(xbc, conv_w, conv_b, first, last):
    t, c = xbc.shape
    tm, tc = CONV_TM, CONV_TC
    rb = tm // SUBLANE
    last_rb = t // SUBLANE - 1
    grid_spec = pltpu.PrefetchScalarGridSpec(
        num_scalar_prefetch=2,
        grid=(t // tm, c // tc),
        in_specs=[
            pl.BlockSpec((SUBLANE, tc), lambda i, j, f, l: (jnp.maximum(i * rb - 1, 0), j)),
            pl.BlockSpec((tm, tc), lambda i, j, f, l: (i, j)),
            pl.BlockSpec((SUBLANE, tc), lambda i, j, f, l: (jnp.minimum((i + 1) * rb, last_rb), j)),
            pl.BlockSpec((D_CONV, tc), lambda i, j, f, l: (0, j)),
            pl.BlockSpec((1, tc), lambda i, j, f, l: (0, j)),
        ],
        out_specs=pl.BlockSpec((tm, tc), lambda i, j, f, l: (i, j)),
        scratch_shapes=[pltpu.VMEM((tm + 2 * SUBLANE, tc), F32)],
    )
    return pl.pallas_call(
        functools.partial(_conv_kernel, tm=tm),
        grid_spec=grid_spec,
        out_shape=jax.ShapeDtypeStruct((t, c), F32),
        compiler_params=_cparams("parallel", "parallel"),
        name="conv_silu",
    )(first, last, xbc, xbc, xbc, conv_w, conv_b.reshape(1, c))


def _ssd_pre_kernel(dt_ref, dtb_ref, alog_ref, dtt_ref, p_ref, pt_ref, w_ref, rows_ref):
    c = CHUNK
    row = lax.broadcasted_iota(jnp.int32, (c, c), 0)
    col = lax.broadcasted_iota(jnp.int32, (c, c), 1)
    tri = (col <= row).astype(F32)
    fwd = col < N_HEADS
    neg_a_rate = -jnp.exp(alog_ref[...])
    for k in range(PRE_CHUNKS):
        rs = slice(k * c, (k + 1) * c)
        dt = _softplus(dt_ref[rs, :] + dtb_ref[...])
        a = dt * neg_a_rate
        cum = jnp.dot(tri, a, precision=lax.Precision.HIGHEST, preferred_element_type=F32)
        total = cum[c - 1:c, :]
        p = jnp.where(fwd, cum, a - cum)
        dte = jnp.exp(jnp.where(fwd, total - p, -p))
        dtt_ref[rs, :] = dt.T
        p_ref[rs, :] = p
        pt_ref[rs, :] = p.T
        w = dt * dte
        w_hi = w.astype(BF16)
        w_ref[rs, :] = jnp.concatenate([w_hi, (w - w_hi.astype(F32)).astype(BF16)], axis=1)
        sub = lax.broadcasted_iota(jnp.int32, (SUBLANE, c), 0)
        off = jnp.where(fwd[:1], 0.0, total)
        rows_ref[k * SUBLANE:(k + 1) * SUBLANE, :] = jnp.where(
            sub == 0, off, jnp.where(sub == 1, jnp.exp(total), 0.0))


def _ssd_pre(dt_raw, dt_bias, a_log):
    t = dt_raw.shape[0]
    rows = PRE_CHUNKS * CHUNK
    blk = pl.BlockSpec((rows, LANE), lambda i: (i, 0))
    const = pl.BlockSpec((1, LANE), lambda i: (0, 0))
    tok = jax.ShapeDtypeStruct((t, LANE), F32)
    return pl.pallas_call(
        _ssd_pre_kernel,
        grid=(t // rows,),
        in_specs=[blk, const, const],
        out_specs=[blk, blk, blk, pl.BlockSpec((rows, 2 * LANE), lambda i: (i, 0)),
                   pl.BlockSpec((PRE_CHUNKS * SUBLANE, LANE), lambda i: (i, 0))],
        out_shape=[tok, tok, tok, jax.ShapeDtypeStruct((t, 2 * LANE), BF16),
                   jax.ShapeDtypeStruct((t // CHUNK * SUBLANE, LANE), F32)],
        compiler_params=_cparams("parallel"),
        name="ssd_pre",
    )(dt_raw, dt_bias, a_log)


def _ssd_kernel(reset_ref, xs_ref, bm_ref, cm_ref, dtt_ref, p_ref, pt_ref, w_ref, rows_ref, sel_ref, *rest, rev):
    if rev:
        yf_ref, z_ref, dskip_ref, nw_ref, o_ref, h_ref, xdec_ref, y_ref = rest
    else:
        o_ref, h_ref, xdec_ref = rest
    c = CHUNK
    step = pl.program_id(0)

    @pl.when(reset_ref[step] == 1)
    def _():
        h_ref[...] = jnp.zeros_like(h_ref)

    row = lax.broadcasted_iota(jnp.int32, (c, c), 0)
    col = lax.broadcasted_iota(jnp.int32, (c, c), 1)
    mask = (row <= col) if rev else (row >= col)
    left = col < HEAD_DIM
    lo = N_HEADS if rev else 0
    p = p_ref[...]
    rows = rows_ref[...]
    off = rows[0:1, :]
    cdec = rows[1:2, :]

    def expand_pair(v, h0, n_rows):
        return jnp.where(left[:n_rows], jnp.broadcast_to(v[:, h0:h0 + 1], (n_rows, LANE)),
                         jnp.broadcast_to(v[:, h0 + 1:h0 + 2], (n_rows, LANE)))

    pairs = GROUP_W // LANE
    for g in range(N_GROUPS):
        gs = slice(g * D_STATE, (g + 1) * D_STATE)
        ws = slice(g * GROUP_W, (g + 1) * GROUP_W)
        w_wide = jnp.dot(w_ref[...], sel_ref[g], preferred_element_type=F32)
        xdec_ref[:, ws] = (xs_ref[:, ws] * w_wide).astype(BF16)
        bm_g = bm_ref[:, gs]
        cm_g = cm_ref[:, gs]
        cb = lax.dot_general(cm_g.astype(BF16), bm_g.astype(BF16), (((1,), (1,)), ((), ())),
                             preferred_element_type=F32)
        h_prev = h_ref[g]
        for jj in range(pairs):
            h0 = lo + g * 2 * pairs + 2 * jj
            sl = slice(g * GROUP_W + jj * LANE, g * GROUP_W + (jj + 1) * LANE)
            rhs = jnp.concatenate([xs_ref[:, sl].astype(BF16),
                                   h_prev[:, jj * LANE:(jj + 1) * LANE].astype(BF16)], axis=0)
            halves = []
            for h in (h0, h0 + 1):
                p_col = jnp.broadcast_to(p[:, h:h + 1], (c, c))
                decay = jnp.exp(jnp.where(mask, p_col - pt_ref[h:h + 1, :], -jnp.inf))
                scores = cb * decay * dtt_ref[h:h + 1, :]
                from_state = cm_g * jnp.exp(p_col + off[:, h:h + 1])
                lhs = jnp.concatenate([scores.astype(BF16), from_state.astype(BF16)], axis=1)
                halves.append(jnp.dot(lhs, rhs, preferred_element_type=F32))
            y_t = jnp.where(left, halves[0], halves[1])
            if rev:
                y_t = yf_ref[:, sl] + y_t + xs_ref[:, sl] * dskip_ref[:, sl]
                y_ref[:, sl] = y_t * _silu(z_ref[:, sl])
            else:
                o_ref[:, sl] = y_t
        chunk_state = jnp.dot(bm_g.T.astype(BF16), xdec_ref[:, ws], preferred_element_type=F32)
        cd_row = jnp.concatenate([expand_pair(cdec, lo + g * 2 * pairs + 2 * jj, 1)
                                  for jj in range(pairs)], axis=1)
        h_ref[g] = h_prev * cd_row + chunk_state
        if rev:
            yg = y_ref[:, ws]
            ms = jnp.mean(yg * yg, axis=-1, keepdims=True)
            o_ref[:, ws] = (yg * lax.rsqrt(ms + EPS) * nw_ref[:, ws]).astype(o_ref.dtype)


def _head_select(rev):
    lo = N_HEADS if rev else 0
    k = np.arange(2 * LANE)[None, :, None] % LANE
    g = np.arange(N_GROUPS)[:, None, None]
    lane = np.arange(GROUP_W)[None, None, :]
    return (k == lo + g * (GROUP_W // HEAD_DIM) + lane // HEAD_DIM).astype(np.float32)


def _ssd(xbc, pre, reset, *, rev, extra=()):
    t = xbc.shape[0]
    nc = t // CHUNK
    if rev:
        cidx = lambda s: nc - 1 - s
    else:
        cidx = lambda s: s
    bm_blk = D_INNER // BC_W
    row = lambda w: pl.BlockSpec((1, w), lambda s, r: (0, 0))
    tok = pl.BlockSpec((CHUNK, LANE), lambda s, r: (cidx(s), 0))
    wide = pl.BlockSpec((CHUNK, D_INNER), lambda s, r: (cidx(s), 0))
    in_specs = [
        wide,
        pl.BlockSpec((CHUNK, BC_W), lambda s, r: (cidx(s), bm_blk)),
        pl.BlockSpec((CHUNK, BC_W), lambda s, r: (cidx(s), bm_blk + 1)),
        tok, tok, tok,
        pl.BlockSpec((CHUNK, 2 * LANE), lambda s, r: (cidx(s), 0)),
        pl.BlockSpec((SUBLANE, LANE), lambda s, r: (cidx(s), 0)),
        pl.BlockSpec((N_GROUPS, 2 * LANE, GROUP_W), lambda s, r: (0, 0, 0)),
    ]
    scratch = [pltpu.VMEM((N_GROUPS, D_STATE, GROUP_W), F32),
               pltpu.VMEM((CHUNK, D_INNER), BF16)]
    args = [reset, xbc, xbc, xbc, *pre, jnp.asarray(_head_select(rev), BF16)]
    if rev:
        in_specs += [wide, wide, row(D_INNER), row(D_INNER)]
        scratch.append(pltpu.VMEM((CHUNK, D_INNER), F32))
        args += list(extra)
        out_dtype = BF16
    else:
        out_dtype = F32
    grid_spec = pltpu.PrefetchScalarGridSpec(
        num_scalar_prefetch=1,
        grid=(nc,),
        in_specs=in_specs,
        out_specs=wide,
        scratch_shapes=scratch,
    )
    return pl.pallas_call(
        functools.partial(_ssd_kernel, rev=rev),
        grid_spec=grid_spec,
        out_shape=jax.ShapeDtypeStruct((t, D_INNER), out_dtype),
        compiler_params=_cparams("arbitrary"),
        name="ssd_bwd" if rev else "ssd_fwd",
    )(*args)


def _gmlp_kernel(uv_ref, nw_ref, nb_ref, sw_ref, sb_ref, o_ref):
    v = _layer_norm(_gelu_tanh(uv_ref[:, D_GMLP:]), nw_ref[...], nb_ref[...])
    for g in range(GMLP_GROUPS):
        sl = slice(g * GMLP_GW, (g + 1) * GMLP_GW)
        v_mix = jnp.dot(sw_ref[g], v[:, sl].astype(BF16), preferred_element_type=F32) + sb_ref[:, sl]
        o_ref[:, sl] = (_gelu_tanh(uv_ref[:, sl]) * v_mix).astype(o_ref.dtype)


def _gmlp(uv, norm_w, norm_b, sgu_w, sgu_b_full):
    t = uv.shape[0]
    const = lambda shape: pl.BlockSpec(shape, lambda i: (0,) * len(shape))
    return pl.pallas_call(
        _gmlp_kernel,
        grid=(t // CHUNK,),
        in_specs=[pl.BlockSpec((CHUNK, 2 * D_GMLP), lambda i: (i, 0)),
                  const((1, D_GMLP)), const((1, D_GMLP)),
                  const((GMLP_GROUPS, CHUNK, CHUNK)), const((CHUNK, D_GMLP))],
        out_specs=pl.BlockSpec((CHUNK, D_GMLP), lambda i: (i, 0)),
        out_shape=jax.ShapeDtypeStruct((t, D_GMLP), BF16),
        compiler_params=_cparams("parallel"),
        name="gmlp",
    )(uv, norm_w.reshape(1, -1), norm_b.reshape(1, -1), sgu_w, sgu_b_full)


def _merge_kernel(ys_ref, yg_ref, wa_ref, wb_ref, ga_ref, gb_ref, o_ref):
    p_a = jnp.dot(ys_ref[...], wa_ref[...], preferred_element_type=F32)
    p_b = jnp.dot(yg_ref[...], wb_ref[...], preferred_element_type=F32)
    merged = jax.nn.sigmoid(ga_ref[...]) * p_a + jax.nn.sigmoid(gb_ref[...]) * p_b
    o_ref[...] = merged.astype(o_ref.dtype)


def _merge(y_ssd, y_gmlp, w_branch, gates):
    t = y_ssd.shape[0]
    tm, tn = MERGE_TM, MERGE_TN
    gb_off = D_MODEL // tn
    return pl.pallas_call(
        _merge_kernel,
        grid=(t // tm, D_MODEL // tn),
        in_specs=[pl.BlockSpec((tm, D_INNER), lambda i, j: (i, 0)),
                  pl.BlockSpec((tm, D_GMLP), lambda i, j: (i, 0)),
                  pl.BlockSpec((D_INNER, tn), lambda i, j: (0, j)),
                  pl.BlockSpec((D_GMLP, tn), lambda i, j: (D_INNER // D_GMLP, j)),
                  pl.BlockSpec((tm, tn), lambda i, j: (i, j)),
                  pl.BlockSpec((tm, tn), lambda i, j: (i, j + gb_off))],
        out_specs=pl.BlockSpec((tm, tn), lambda i, j: (i, j)),
        out_shape=jax.ShapeDtypeStruct((t, D_MODEL), BF16),
        compiler_params=_cparams("parallel", "parallel"),
        name="branch_merge",
    )(y_ssd, y_gmlp, w_branch, w_branch, gates, gates)


def _outproj_ln_kernel(m_ref, w_ref, x_ref, g_ref, b_ref, o_ref, ob_ref):
    mix = jnp.dot(m_ref[...], w_ref[...], preferred_element_type=F32)
    y = _layer_norm(ALPHA * x_ref[...] + mix, g_ref[...], b_ref[...])
    o_ref[...] = y
    ob_ref[...] = y.astype(BF16)


def _outproj_ln(merged, w_out, x, g, b):
    t = x.shape[0]
    tm = LN_TM
    rowspec = pl.BlockSpec((tm, D_MODEL), lambda i: (i, 0))
    const = lambda shape: pl.BlockSpec(shape, lambda i: (0, 0))
    return pl.pallas_call(
        _outproj_ln_kernel,
        grid=(t // tm,),
        in_specs=[rowspec, const((D_MODEL, D_MODEL)), rowspec, const((1, D_MODEL)), const((1, D_MODEL))],
        out_specs=[rowspec, rowspec],
        out_shape=[jax.ShapeDtypeStruct((t, D_MODEL), F32), jax.ShapeDtypeStruct((t, D_MODEL), BF16)],
        compiler_params=_cparams("parallel"),
        name="outproj_ln1",
    )(merged, w_out, x, g.reshape(1, -1), b.reshape(1, -1))


def _router_kernel(x_ref, w_ref, b_ref, o_ref):
    lg = jnp.dot(x_ref[...], w_ref[...], precision=lax.Precision.HIGHEST,
                 preferred_element_type=F32) + b_ref[...]
    lane = lax.broadcasted_iota(jnp.int32, lg.shape, 1)
    big = jnp.int32(ROUTE_W)
    neg = -jnp.inf
    g_l = jnp.where(lane < N_EGROUPS, lg, neg)
    g_max = jnp.max(g_l, axis=-1, keepdims=True)
    g_sum = jnp.sum(jnp.exp(g_l - g_max), axis=-1, keepdims=True)
    g_w = 1.0 / g_sum
    g_idx = jnp.min(jnp.where(g_l == g_max, lane, big), axis=-1, keepdims=True)
    e_lo = N_EGROUPS + g_idx * E_PER_GROUP
    e_l = jnp.where(lane >= e_lo, jnp.where(lane < e_lo + E_PER_GROUP, lg, neg), neg)
    e_max = jnp.max(e_l, axis=-1, keepdims=True)
    e_sum = jnp.sum(jnp.exp(e_l - e_max), axis=-1, keepdims=True)
    i1 = jnp.min(jnp.where(e_l == e_max, lane, big), axis=-1, keepdims=True)
    e_l2 = jnp.where(lane == i1, neg, e_l)
    e_max2 = jnp.max(e_l2, axis=-1, keepdims=True)
    i2 = jnp.min(jnp.where(e_l2 == e_max2, lane, big), axis=-1, keepdims=True)
    p1 = 1.0 / e_sum
    p2 = jnp.exp(e_max2 - e_max) / e_sum
    w1 = g_w * (p1 / (p1 + p2))
    w2 = g_w * (p2 / (p1 + p2))
    id1 = (i1 - N_EGROUPS).astype(F32)
    id2 = (i2 - N_EGROUPS).astype(F32)
    out = jnp.where(lane == 0, id1, jnp.where(lane == 1, id2, jnp.where(lane == 2, w1, jnp.where(lane == 3, w2, 0.0))))
    o_ref[...] = out


def _router(x, w_r, b_r):
    t = x.shape[0]
    tm = ROUTE_TM
    return pl.pallas_call(
        _router_kernel,
        grid=(t // tm,),
        in_specs=[pl.BlockSpec((tm, D_MODEL), lambda i: (i, 0)),
                  pl.BlockSpec((D_MODEL, ROUTE_W), lambda i: (0, 0)),
                  pl.BlockSpec((1, ROUTE_W), lambda i: (0, 0))],
        out_specs=pl.BlockSpec((tm, ROUTE_W), lambda i: (i, 0)),
        out_shape=jax.ShapeDtypeStruct((t, ROUTE_W), F32),
        compiler_params=_cparams("parallel"),
        name="router",
    )(x, w_r, b_r)


DMA_UNROLL = 8


def _rows_copy(src, dst, sem, n_rows):
    return pltpu.make_async_copy(src.at[pl.ds(0, n_rows), :], dst.at[pl.ds(0, n_rows), :], sem)


def _wait_rows(src, dst, sem, n):
    n_tiled = pl.multiple_of(n // SUBLANE * SUBLANE, SUBLANE)

    @pl.when(n_tiled > 0)
    def _():
        _rows_copy(src, dst, sem, n_tiled).wait()

    def body(r, carry):
        _rows_copy(src, dst, sem, 1).wait()
        return carry
    lax.fori_loop(0, n - n_tiled, body, 0)


def _for_rows(n, fn):
    def group(i, carry):
        for u in range(DMA_UNROLL):
            fn(i * DMA_UNROLL + u, u % 2)
        return carry

    def single(r, carry):
        fn(r, 0)
        return carry
    n_groups = n // DMA_UNROLL
    lax.fori_loop(0, n_groups, group, 0)
    lax.fori_loop(n_groups * DMA_UNROLL, n, single, 0)


def _expert_kernel(be_ref, tok_ref, dst_ref, nv_ref, x_hbm, wg_ref, wu_ref, wd_ref, y_hbm,
                   xbuf0, xbuf1, obuf0, obuf1, wg_b, wu_b, wd_b, gsem, ssem, *, n_blocks):
    b = pl.program_id(0)
    blk = EXPERT_BLK
    xbufs = (xbuf0, xbuf1)
    obufs = (obuf0, obuf1)

    def gather(block, s):
        def fetch(r, priority):
            tok = tok_ref[block * blk + r]
            pltpu.make_async_copy(x_hbm.at[pl.ds(tok, 1), :], xbufs[s].at[pl.ds(r, 1), :],
                                  gsem.at[s]).start(priority=priority)
        _for_rows(nv_ref[block], fetch)

    def scatter(block, s):
        def send(r, priority):
            row = dst_ref[block * blk + r]
            pltpu.make_async_copy(obufs[s].at[pl.ds(r, 1), :], y_hbm.at[pl.ds(row, 1), :],
                                  ssem.at[s]).start(priority=priority)
        _for_rows(nv_ref[block], send)

    @pl.when(b == 0)
    def _():
        for s in range(2):
            xbufs[s][...] = jnp.zeros_like(xbufs[s])
        gather(0, 0)

    @pl.when((b == 0) | (be_ref[b] != be_ref[jnp.maximum(b - 1, 0)]))
    def _():
        wg_b[...] = wg_ref[...].astype(BF16)
        wu_b[...] = wu_ref[...].astype(BF16)
        wd_b[...] = wd_ref[...].astype(BF16)

    def step(s):
        @pl.when(b + 1 < n_blocks)
        def _():
            gather(jnp.minimum(b + 1, n_blocks - 1), 1 - s)

        _wait_rows(x_hbm, xbufs[s], gsem.at[s], nv_ref[b])

        @pl.when(b >= 2)
        def _():
            _wait_rows(obufs[s], y_hbm, ssem.at[s], nv_ref[jnp.maximum(b - 2, 0)])

        @pl.when(nv_ref[b] > 0)
        def _():
            x = xbufs[s][...].astype(BF16)
            h = _silu(jnp.dot(x, wg_b[...], preferred_element_type=F32)) * \
                jnp.dot(x, wu_b[...], preferred_element_type=F32)
            obufs[s][...] = jnp.dot(h.astype(BF16), wd_b[...], preferred_element_type=F32)
            scatter(b, s)

    for s in range(2):
        pl.when(b % 2 == s)(functools.partial(step, s))

    @pl.when(b == n_blocks - 1)
    def _():
        for back in range(2):
            s = (n_blocks - 1 - back) % 2
            _wait_rows(obufs[s], y_hbm, ssem.at[s], nv_ref[n_blocks - 1 - back])


def _experts(x, w_gate, w_up, w_down, layer, block_expert, slot_tok, slot_dst, n_valid):
    nb = block_expert.shape[0]
    t = x.shape[0]
    assert nb >= 2
    wspec = lambda r, c: pl.BlockSpec((None, None, r, c), lambda b, be, st, sd, nv: (layer, be[b], 0, 0))
    row_buf = pltpu.VMEM((EXPERT_BLK, D_MODEL), F32)
    grid_spec = pltpu.PrefetchScalarGridSpec(
        num_scalar_prefetch=4,
        grid=(nb,),
        in_specs=[
            pl.BlockSpec(memory_space=pl.ANY),
            wspec(D_MODEL, D_EXPERT), wspec(D_MODEL, D_EXPERT), wspec(D_EXPERT, D_MODEL),
        ],
        out_specs=pl.BlockSpec(memory_space=pl.ANY),
        scratch_shapes=[row_buf, row_buf, row_buf, row_buf,
                        pltpu.VMEM((D_MODEL, D_EXPERT), BF16),
                        pltpu.VMEM((D_MODEL, D_EXPERT), BF16),
                        pltpu.VMEM((D_EXPERT, D_MODEL), BF16),
                        pltpu.SemaphoreType.DMA((2,)),
                        pltpu.SemaphoreType.DMA((2,))],
    )
    return pl.pallas_call(
        functools.partial(_expert_kernel, n_blocks=nb),
        grid_spec=grid_spec,
        out_shape=jax.ShapeDtypeStruct((TOP_K * t, D_MODEL), F32),
        compiler_params=_cparams("arbitrary", disable_bounds_checks=True),
        name="experts",
    )(block_expert, slot_tok, slot_dst, n_valid, x, w_gate, w_up, w_down)


def _combine_ln_kernel(y1_ref, y2_ref, route_ref, x_ref, g_ref, b_ref, o_ref, ob_ref):
    route = route_ref[...]
    ffn = y1_ref[...] * route[:, 2:3] + y2_ref[...] * route[:, 3:4]
    y = _layer_norm(ALPHA * x_ref[...] + ffn, g_ref[...], b_ref[...])
    o_ref[...] = y
    ob_ref[...] = y.astype(BF16)


def _combine_ln(y_asg, route, x, g, b):
    t = x.shape[0]
    tm = LN_TM
    rowspec = pl.BlockSpec((tm, D_MODEL), lambda i: (i, 0))
    const = pl.BlockSpec((1, D_MODEL), lambda i: (0, 0))
    return pl.pallas_call(
        _combine_ln_kernel,
        grid=(t // tm,),
        in_specs=[rowspec,
                  pl.BlockSpec((tm, D_MODEL), lambda i: (i + t // tm, 0)),
                  pl.BlockSpec((tm, ROUTE_W), lambda i: (i, 0)),
                  rowspec, const, const],
        out_specs=[rowspec, rowspec],
        out_shape=[jax.ShapeDtypeStruct((t, D_MODEL), F32), jax.ShapeDtypeStruct((t, D_MODEL), BF16)],
        compiler_params=_cparams("parallel"),
        name="combine_ln2",
    )(y_asg, y_asg, route, x, g.reshape(1, -1), b.reshape(1, -1))


def _routing_tables(route, t):
    a_n = t * TOP_K
    e_flat = route[:, 0:TOP_K].astype(jnp.int32).T.reshape(a_n)
    order = jnp.argsort(e_flat).astype(jnp.int32)
    counts = jnp.sum((e_flat[:, None] == jnp.arange(N_EXPERTS, dtype=jnp.int32)[None, :]).astype(jnp.int32), axis=0)
    padded = (counts + EXPERT_BLK - 1) // EXPERT_BLK * EXPERT_BLK
    starts = jnp.cumsum(counts) - counts
    pends = jnp.cumsum(padded)
    pstarts = pends - padded
    n_blocks = a_n // EXPERT_BLK + N_EXPERTS
    block = jnp.arange(n_blocks, dtype=jnp.int32)
    block_start = block * EXPERT_BLK
    block_expert = jnp.minimum(jnp.sum((block_start[:, None] >= pends[None, :]).astype(jnp.int32), axis=1),
                               N_EXPERTS - 1).astype(jnp.int32)
    in_seg = block_start - pstarts[block_expert]
    n_valid = jnp.clip(counts[block_expert] - in_seg, 0, EXPERT_BLK)
    r = jnp.arange(EXPERT_BLK, dtype=jnp.int32)[None, :]
    src = (starts[block_expert] + in_seg)[:, None] + r
    valid = r < n_valid[:, None]
    asg = jnp.where(valid, order[jnp.clip(src, 0, a_n - 1)], 0)
    slot_tok = jnp.where(asg >= t, asg - t, asg).reshape(-1).astype(jnp.int32)
    slot_dst = asg.reshape(-1).astype(jnp.int32)
    return slot_tok, slot_dst, block_expert, n_valid.astype(jnp.int32)


def _seq_flags(groups, tile):
    first, last = [], []
    for batch, length in groups:
        n = length // tile
        for _ in range(batch):
            first += [1] + [0] * (n - 1)
            last += [0] * (n - 1) + [1]
    return np.asarray(first, np.int32), np.asarray(last, np.int32)


def _trunk(x, xb, groups, w_in, conv_w, conv_b, dt_bias, a_log, d_skip, ssd_norm_w, sgu_norm_w, sgu_norm_b,
           sgu_w, sgu_b, w_branch, w_out, ln1_g, ln1_b, ln2_g, ln2_b,
           w_router_group, b_router_group, w_router_expert, b_router_expert, w_gate, w_up, w_down):
    t = x.shape[0]
    depth = w_in.shape[0]
    conv_first, conv_last = _seq_flags(groups, CONV_TM)
    chunk_first, chunk_last = _seq_flags(groups, CHUNK)
    reset_f = jnp.asarray(chunk_first)
    reset_b = jnp.asarray(chunk_last[::-1].copy())
    conv_first, conv_last = jnp.asarray(conv_first), jnp.asarray(conv_last)
    pad_r = ROUTE_W - N_EGROUPS - N_EXPERTS
    for i in range(depth):
        z = _proj(xb, w_in, i, 0, OFF_XBC, MM_TM, MM_TN, "proj_z")
        xbc = _proj(xb, w_in, i, OFF_XBC, CONV_DIM, MM_TM, MM_TN, "proj_xbc")
        dt_raw = _proj(xb, w_in, i, OFF_DT, 2 * N_HEADS, MM_TM, LANE, "proj_dt")
        uv = _proj(xb, w_in, i, OFF_UV, 2 * D_GMLP, MM_TM, MM_TN, "proj_uv")
        gates = _proj(xb, w_in, i, OFF_GATE, 2 * D_MODEL, MM_TM, MM_TN, "proj_gates")

        xbc = _conv_silu(xbc, conv_w[i], conv_b[i], conv_first, conv_last)
        pre = _ssd_pre(dt_raw, dt_bias[i].reshape(1, 2 * N_HEADS), a_log[i].reshape(1, 2 * N_HEADS))
        y_f = _ssd(xbc, pre, reset_f, rev=False)
        dskip = jnp.repeat(d_skip[i], HEAD_DIM).reshape(1, D_INNER)
        y_ssd = _ssd(xbc, pre, reset_b, rev=True,
                     extra=(y_f, z, dskip, ssd_norm_w[i].reshape(1, D_INNER)))

        sgu_b_full = jnp.repeat(sgu_b[i].T, GMLP_GW, axis=1)
        y_gmlp = _gmlp(uv, sgu_norm_w[i], sgu_norm_b[i], sgu_w[i].astype(BF16), sgu_b_full)

        merged = _merge(y_ssd, y_gmlp, w_branch[i].astype(BF16), gates)
        x, xb = _outproj_ln(merged, w_out[i].astype(BF16), x, ln1_g[i], ln1_b[i])

        w_r = jnp.concatenate([w_router_group[i], w_router_expert[i],
                               jnp.zeros((D_MODEL, pad_r), F32)], axis=1)
        b_r = jnp.concatenate([b_router_group[i], b_router_expert[i], jnp.zeros((pad_r,), F32)]).reshape(1, ROUTE_W)
        route = _router(x, w_r, b_r)
        slot_tok, slot_dst, block_expert, n_valid = _routing_tables(route, t)
        y_asg = _experts(x, w_gate, w_up, w_down, i, block_expert, slot_tok, slot_dst, n_valid)
        x, xb = _combine_ln(y_asg, route, x, ln2_g[i], ln2_b[i])
    return x


def kernel(x_prompt, x_sample, w_in, conv_w, conv_b, dt_bias, a_log, d_skip, ssd_norm_w, sgu_norm_w, sgu_norm_b, sgu_w, sgu_b, w_branch, w_out, ln1_g, ln1_b, ln2_g, ln2_b, w_router_group, b_router_group, w_router_expert, b_router_expert, w_gate, w_up, w_down):
    groups = (x_prompt.shape[:2], x_sample.shape[:2])
    for _, length in groups:
        assert length % CONV_TM == 0 and length % CHUNK == 0
    x = jnp.concatenate([x_prompt.reshape(-1, D_MODEL), x_sample.reshape(-1, D_MODEL)], axis=0)
    t = x.shape[0]
    assert t % MM_TM == 0 and (t * TOP_K) % EXPERT_BLK == 0 and t % (PRE_CHUNKS * CHUNK) == 0
    y = _trunk(x, x.astype(BF16), groups, w_in, conv_w, conv_b, dt_bias, a_log, d_skip, ssd_norm_w,
               sgu_norm_w, sgu_norm_b, sgu_w, sgu_b, w_branch, w_out, ln1_g, ln1_b, ln2_g, ln2_b,
               w_router_group, b_router_group, w_router_expert, b_router_expert, w_gate, w_up, w_down)
    n_p = x_prompt.shape[0] * x_prompt.shape[1]
    return (y[:n_p].reshape(x_prompt.shape), y[n_p:].reshape(x_sample.shape))
```

```python
import functools
import math

import numpy as np
import jax
import jax.numpy as jnp
from jax import lax
from jax.experimental import pallas as pl
from jax.experimental.pallas import tpu as pltpu

F32 = jnp.float32
BF16 = jnp.bfloat16

D_MODEL = 2048
DEPTH = 4
D_INNER = 2 * D_MODEL
HEAD_DIM = 64
N_HEADS = D_INNER // HEAD_DIM
N_GROUPS = 8
GROUP_W = D_INNER // N_GROUPS
D_STATE = 128
D_CONV = 5
CONV_PAD = D_CONV // 2
BC_W = N_GROUPS * D_STATE
CONV_DIM = D_INNER + 2 * BC_W
CHUNK = 128
D_GMLP = D_MODEL
GMLP_GROUPS = 8
GMLP_GW = D_GMLP // GMLP_GROUPS
OFF_XBC = D_INNER
OFF_DT = OFF_XBC + CONV_DIM
OFF_UV = OFF_DT + 2 * N_HEADS
OFF_GATE = OFF_UV + 2 * D_GMLP
N_IN = OFF_GATE + 2 * D_MODEL
N_EGROUPS = 8
E_PER_GROUP = 8
N_EXPERTS = N_EGROUPS * E_PER_GROUP
TOP_K = 2
D_EXPERT = D_MODEL // 4
ALPHA = (2.0 * DEPTH) ** 0.25
EPS = 1e-5

LANE = 128
SUBLANE = 8
V7X_VMEM_BYTES = 64 * 1024 * 1024
VMEM_LIMIT = V7X_VMEM_BYTES * 3 // 4

MM_TM = 1024
MM_TN = 1024
CONV_TM = 512
CONV_TC = 512
PRE_CHUNKS = 8
MERGE_TM = 512
MERGE_TN = 512
LN_TM = 256
ROUTE_TM = 512
EXPERT_BLK = 256
ROUTE_W = LANE


def _cparams(*sem, **kw):
    return pltpu.CompilerParams(dimension_semantics=sem, vmem_limit_bytes=VMEM_LIMIT, **kw)


def _silu(x):
    return x * jax.nn.sigmoid(x)


def _softplus(x):
    return jnp.maximum(x, 0.0) + jnp.log1p(jnp.exp(-jnp.abs(x)))


def _gelu_tanh(x):
    c = math.sqrt(2.0 / math.pi)
    return x * (0.5 * (1.0 + jnp.tanh(c * (x + 0.044715 * (x * x * x)))))


def _layer_norm(v, g, b):
    mu = jnp.mean(v, axis=-1, keepdims=True)
    xc = v - mu
    var = jnp.mean(xc * xc, axis=-1, keepdims=True)
    return xc * lax.rsqrt(var + EPS) * g + b


def _proj_kernel(x_ref, w_hbm, o_ref, w_f32, w_bf, sem, *, layer, col_off, tn):
    @pl.when(pl.program_id(1) == 0)
    def _():
        col = pl.multiple_of(col_off + pl.program_id(0) * tn, LANE)
        copy = pltpu.make_async_copy(w_hbm.at[layer, :, pl.ds(col, tn)], w_f32, sem)
        copy.start()
        copy.wait()
        w_bf[...] = w_f32[...].astype(BF16)

    o_ref[...] = jnp.dot(x_ref[...], w_bf[...], preferred_element_type=F32)


def _proj(x, w_in, layer, col_off, n, tm, tn, name):
    m, k = x.shape
    return pl.pallas_call(
        functools.partial(_proj_kernel, layer=layer, col_off=col_off, tn=tn),
        grid=(n // tn, m // tm),
        in_specs=[pl.BlockSpec((tm, k), lambda j, i: (i, 0)),
                  pl.BlockSpec(memory_space=pl.ANY)],
        out_specs=pl.BlockSpec((tm, tn), lambda j, i: (i, j)),
        out_shape=jax.ShapeDtypeStruct((m, n), F32),
        scratch_shapes=[pltpu.VMEM((k, tn), F32), pltpu.VMEM((k, tn), BF16), pltpu.SemaphoreType.DMA],
        compiler_params=_cparams("arbitrary", "arbitrary"),
        name=name,
    )(x, w_in)


def _conv_kernel(first_ref, last_ref, xp_ref, x_ref, xn_ref, w_ref, b_ref, o_ref, ext_ref, *, tm):
    i = pl.program_id(0)
    keep_prev = jnp.where(first_ref[i] == 1, 0.0, 1.0).astype(F32)
    keep_next = jnp.where(last_ref[i] == 1, 0.0, 1.0).astype(F32)
    ext_ref[0:SUBLANE, :] = xp_ref[...] * keep_prev
    ext_ref[SUBLANE:SUBLANE + tm, :] = x_ref[...]
    ext_ref[SUBLANE + tm:2 * SUBLANE + tm, :] = xn_ref[...] * keep_next
    ext = ext_ref[...]
    n_ext = tm + 2 * SUBLANE
    acc = jnp.broadcast_to(b_ref[...], (tm, b_ref.shape[1]))
    for k in range(D_CONV):
        shift = (CONV_PAD - k) % n_ext
        tap = ext if shift == 0 else pltpu.roll(ext, shift, 0)
        acc = acc + w_ref[k:k + 1, :] * tap[SUBLANE:SUBLANE + tm, :]
    o_ref[...] = _silu(acc)


def _conv_silu(xbc, conv_w, conv_b, first, last):
    t, c = xbc.shape
    tm, tc = CONV_TM, CONV_TC
    rb = tm // SUBLANE
    last_rb = t // SUBLANE - 1
    grid_spec = pltpu.PrefetchScalarGridSpec(
        num_scalar_prefetch=2,
        grid=(t // tm, c // tc),
        in_specs=[
            pl.BlockSpec((SUBLANE, tc), lambda i, j, f, l: (jnp.maximum(i * rb - 1, 0), j)),
            pl.BlockSpec((tm, tc), lambda i, j, f, l: (i, j)),
            pl.BlockSpec((SUBLANE, tc), lambda i, j, f, l: (jnp.minimum((i + 1) * rb, last_rb), j)),
            pl.BlockSpec((D_CONV, tc), lambda i, j, f, l: (0, j)),
            pl.BlockSpec((1, tc), lambda i, j, f, l: (0, j)),
        ],
        out_specs=pl.BlockSpec((tm, tc), lambda i, j, f, l: (i, j)),
        scratch_shapes=[pltpu.VMEM((tm + 2 * SUBLANE, tc), F32)],
    )
    return pl.pallas_call(
        functools.partial(_conv_kernel, tm=tm),
        grid_spec=grid_spec,
        out_shape=jax.ShapeDtypeStruct((t, c), F32),
        compiler_params=_cparams("parallel", "parallel"),
        name="conv_silu",
    )(first, last, xbc, xbc, xbc, conv_w, conv_b.reshape(1, c))


def _ssd_pre_kernel(dt_ref, dtb_ref, alog_ref, dtt_ref, p_ref, pt_ref, w_ref, rows_ref):
    c = CHUNK
    row = lax.broadcasted_iota(jnp.int32, (c, c), 0)
    col = lax.broadcasted_iota(jnp.int32, (c, c), 1)
    tri = (col <= row).astype(F32)
    fwd = col < N_HEADS
    neg_a_rate = -jnp.exp(alog_ref[...])
    for k in range(PRE_CHUNKS):
        rs = slice(k * c, (k + 1) * c)
        dt = _softplus(dt_ref[rs, :] + dtb_ref[...])
        a = dt * neg_a_rate
        cum = jnp.dot(tri, a, precision=lax.Precision.HIGHEST, preferred_element_type=F32)
        total = cum[c - 1:c, :]
        p = jnp.where(fwd, cum, a - cum)
        dte = jnp.exp(jnp.where(fwd, total - p, -p))
        dtt_ref[rs, :] = dt.T
        p_ref[rs, :] = p
        pt_ref[rs, :] = p.T
        w = dt * dte
        w_hi = w.astype(BF16)
        w_ref[rs, :] = jnp.concatenate([w_hi, (w - w_hi.astype(F32)).astype(BF16)], axis=1)
        sub = lax.broadcasted_iota(jnp.int32, (SUBLANE, c), 0)
        off = jnp.where(fwd[:1], 0.0, total)
        rows_ref[k * SUBLANE:(k + 1) * SUBLANE, :] = jnp.where(
            sub == 0, off, jnp.where(sub == 1, jnp.exp(total), 0.0))


def _ssd_pre(dt_raw, dt_bias, a_log):
    t = dt_raw.shape[0]
    rows = PRE_CHUNKS * CHUNK
    blk = pl.BlockSpec((rows, LANE), lambda i: (i, 0))
    const = pl.BlockSpec((1, LANE), lambda i: (0, 0))
    tok = jax.ShapeDtypeStruct((t, LANE), F32)
    return pl.pallas_call(
        _ssd_pre_kernel,
        grid=(t // rows,),
        in_specs=[blk, const, const],
        out_specs=[blk, blk, blk, pl.BlockSpec((rows, 2 * LANE), lambda i: (i, 0)),
                   pl.BlockSpec((PRE_CHUNKS * SUBLANE, LANE), lambda i: (i, 0))],
        out_shape=[tok, tok, tok, jax.ShapeDtypeStruct((t, 2 * LANE), BF16),
                   jax.ShapeDtypeStruct((t // CHUNK * SUBLANE, LANE), F32)],
        compiler_params=_cparams("parallel"),
        name="ssd_pre",
    )(dt_raw, dt_bias, a_log)


def _ssd_kernel(reset_ref, xs_ref, bm_ref, cm_ref, dtt_ref, p_ref, pt_ref, w_ref, rows_ref, sel_ref, *rest, rev):
    if rev:
        yf_ref, z_ref, dskip_ref, nw_ref, o_ref, h_ref, xdec_ref, y_ref = rest
    else:
        o_ref, h_ref, xdec_ref = rest
    c = CHUNK
    step = pl.program_id(0)

    @pl.when(reset_ref[step] == 1)
    def _():
        h_ref[...] = jnp.zeros_like(h_ref)

    row = lax.broadcasted_iota(jnp.int32, (c, c), 0)
    col = lax.broadcasted_iota(jnp.int32, (c, c), 1)
    mask = (row <= col) if rev else (row >= col)
    left = col < HEAD_DIM
    lo = N_HEADS if rev else 0
    p = p_ref[...]
    rows = rows_ref[...]
    off = rows[0:1, :]
    cdec = rows[1:2, :]

    def expand_pair(v, h0, n_rows):
        return jnp.where(left[:n_rows], jnp.broadcast_to(v[:, h0:h0 + 1], (n_rows, LANE)),
                         jnp.broadcast_to(v[:, h0 + 1:h0 + 2], (n_rows, LANE)))

    pairs = GROUP_W // LANE
    for g in range(N_GROUPS):
        gs = slice(g * D_STATE, (g + 1) * D_STATE)
        ws = slice(g * GROUP_W, (g + 1) * GROUP_W)
        w_wide = jnp.dot(w_ref[...], sel_ref[g], preferred_element_type=F32)
        xdec_ref[:, ws] = (xs_ref[:, ws] * w_wide).astype(BF16)
        bm_g = bm_ref[:, gs]
        cm_g = cm_ref[:, gs]
        cb = lax.dot_general(cm_g.astype(BF16), bm_g.astype(BF16), (((1,), (1,)), ((), ())),
                             preferred_element_type=F32)
        h_prev = h_ref[g]
        for jj in range(pairs):
            h0 = lo + g * 2 * pairs + 2 * jj
            sl = slice(g * GROUP_W + jj * LANE, g * GROUP_W + (jj + 1) * LANE)
            rhs = jnp.concatenate([xs_ref[:, sl].astype(BF16),
                                   h_prev[:, jj * LANE:(jj + 1) * LANE].astype(BF16)], axis=0)
            halves = []
            for h in (h0, h0 + 1):
                p_col = jnp.broadcast_to(p[:, h:h + 1], (c, c))
                decay = jnp.exp(jnp.where(mask, p_col - pt_ref[h:h + 1, :], -jnp.inf))
                scores = cb * decay * dtt_ref[h:h + 1, :]
                from_state = cm_g * jnp.exp(p_col + off[:, h:h + 1])
                lhs = jnp.concatenate([scores.astype(BF16), from_state.astype(BF16)], axis=1)
                halves.append(jnp.dot(lhs, rhs, preferred_element_type=F32))
            y_t = jnp.where(left, halves[0], halves[1])
            if rev:
                y_t = yf_ref[:, sl] + y_t + xs_ref[:, sl] * dskip_ref[:, sl]
                y_ref[:, sl] = y_t * _silu(z_ref[:, sl])
            else:
                o_ref[:, sl] = y_t
        chunk_state = jnp.dot(bm_g.T.astype(BF16), xdec_ref[:, ws], preferred_element_type=F32)
        cd_row = jnp.concatenate([expand_pair(cdec, lo + g * 2 * pairs + 2 * jj, 1)
                                  for jj in range(pairs)], axis=1)
        h_ref[g] = h_prev * cd_row + chunk_state
        if rev:
            yg = y_ref[:, ws]
            ms = jnp.mean(yg * yg, axis=-1, keepdims=True)
            o_ref[:, ws] = (yg * lax.rsqrt(ms + EPS) * nw_ref[:, ws]).astype(o_ref.dtype)


def _head_select(rev):
    lo = N_HEADS if rev else 0
    k = np.arange(2 * LANE)[None, :, None] % LANE
    g = np.arange(N_GROUPS)[:, None, None]
    lane = np.arange(GROUP_W)[None, None, :]
    return (k == lo + g * (GROUP_W // HEAD_DIM) + lane // HEAD_DIM).astype(np.float32)


def _ssd(xbc, pre, reset, *, rev, extra=()):
    t = xbc.shape[0]
    nc = t // CHUNK
    if rev:
        cidx = lambda s: nc - 1 - s
    else:
        cidx = lambda s: s
    bm_blk = D_INNER // BC_W
    row = lambda w: pl.BlockSpec((1, w), lambda s, r: (0, 0))
    tok = pl.BlockSpec((CHUNK, LANE), lambda s, r: (cidx(s), 0))
    wide = pl.BlockSpec((CHUNK, D_INNER), lambda s, r: (cidx(s), 0))
    in_specs = [
        wide,
        pl.BlockSpec((CHUNK, BC_W), lambda s, r: (cidx(s), bm_blk)),
        pl.BlockSpec((CHUNK, BC_W), lambda s, r: (cidx(s), bm_blk + 1)),
        tok, tok, tok,
        pl.BlockSpec((CHUNK, 2 * LANE), lambda s, r: (cidx(s), 0)),
        pl.BlockSpec((SUBLANE, LANE), lambda s, r: (cidx(s), 0)),
        pl.BlockSpec((N_GROUPS, 2 * LANE, GROUP_W), lambda s, r: (0, 0, 0)),
    ]
    scratch = [pltpu.VMEM((N_GROUPS, D_STATE, GROUP_W), F32),
               pltpu.VMEM((CHUNK, D_INNER), BF16)]
    args = [reset, xbc, xbc, xbc, *pre, jnp.asarray(_head_select(rev), BF16)]
    if rev:
        in_specs += [wide, wide, row(D_INNER), row(D_INNER)]
        scratch.append(pltpu.VMEM((CHUNK, D_INNER), F32))
        args += list(extra)
        out_dtype = BF16
    else:
        out_dtype = F32
    grid_spec = pltpu.PrefetchScalarGridSpec(
        num_scalar_prefetch=1,
        grid=(nc,),
        in_specs=in_specs,
        out_specs=wide,
        scratch_shapes=scratch,
    )
    return pl.pallas_call(
        functools.partial(_ssd_kernel, rev=rev),
        grid_spec=grid_spec,
        out_shape=jax.ShapeDtypeStruct((t, D_INNER), out_dtype),
        compiler_params=_cparams("arbitrary"),
        name="ssd_bwd" if rev else "ssd_fwd",
    )(*args)


def _gmlp_kernel(uv_ref, nw_ref, nb_ref, sw_ref, sb_ref, o_ref):
    v = _layer_norm(_gelu_tanh(uv_ref[:, D_GMLP:]), nw_ref[...], nb_ref[...])
    for g in range(GMLP_GROUPS):
        sl = slice(g * GMLP_GW, (g + 1) * GMLP_GW)
        v_mix = jnp.dot(sw_ref[g], v[:, sl].astype(BF16), preferred_element_type=F32) + sb_ref[:, sl]
        o_ref[:, sl] = (_gelu_tanh(uv_ref[:, sl]) * v_mix).astype(o_ref.dtype)


def _gmlp(uv, norm_w, norm_b, sgu_w, sgu_b_full):
    t = uv.shape[0]
    const = lambda shape: pl.BlockSpec(shape, lambda i: (0,) * len(shape))
    return pl.pallas_call(
        _gmlp_kernel,
        grid=(t // CHUNK,),
        in_specs=[pl.BlockSpec((CHUNK, 2 * D_GMLP), lambda i: (i, 0)),
                  const((1, D_GMLP)), const((1, D_GMLP)),
                  const((GMLP_GROUPS, CHUNK, CHUNK)), const((CHUNK, D_GMLP))],
        out_specs=pl.BlockSpec((CHUNK, D_GMLP), lambda i: (i, 0)),
        out_shape=jax.ShapeDtypeStruct((t, D_GMLP), BF16),
        compiler_params=_cparams("parallel"),
        name="gmlp",
    )(uv, norm_w.reshape(1, -1), norm_b.reshape(1, -1), sgu_w, sgu_b_full)


def _merge_kernel(ys_ref, yg_ref, wa_ref, wb_ref, ga_ref, gb_ref, o_ref):
    p_a = jnp.dot(ys_ref[...], wa_ref[...], preferred_element_type=F32)
    p_b = jnp.dot(yg_ref[...], wb_ref[...], preferred_element_type=F32)
    merged = jax.nn.sigmoid(ga_ref[...]) * p_a + jax.nn.sigmoid(gb_ref[...]) * p_b
    o_ref[...] = merged.astype(o_ref.dtype)


def _merge(y_ssd, y_gmlp, w_branch, gates):
    t = y_ssd.shape[0]
    tm, tn = MERGE_TM, MERGE_TN
    gb_off = D_MODEL // tn
    return pl.pallas_call(
        _merge_kernel,
        grid=(t // tm, D_MODEL // tn),
        in_specs=[pl.BlockSpec((tm, D_INNER), lambda i, j: (i, 0)),
                  pl.BlockSpec((tm, D_GMLP), lambda i, j: (i, 0)),
                  pl.BlockSpec((D_INNER, tn), lambda i, j: (0, j)),
                  pl.BlockSpec((D_GMLP, tn), lambda i, j: (D_INNER // D_GMLP, j)),
                  pl.BlockSpec((tm, tn), lambda i, j: (i, j)),
                  pl.BlockSpec((tm, tn), lambda i, j: (i, j + gb_off))],
        out_specs=pl.BlockSpec((tm, tn), lambda i, j: (i, j)),
        out_shape=jax.ShapeDtypeStruct((t, D_MODEL), BF16),
        compiler_params=_cparams("parallel", "parallel"),
        name="branch_merge",
    )(y_ssd, y_gmlp, w_branch, w_branch, gates, gates)


def _pack_bf16_pairs(y):
    n = y.shape[1] // 2
    hi = lax.bitcast_convert_type(y[:, :n].astype(BF16).astype(F32), jnp.uint32)
    lo = lax.bitcast_convert_type(y[:, n:].astype(BF16).astype(F32), jnp.uint32)
    return hi | (lo >> 16)


def _unpack_bf16_pairs(w):
    left = lax.bitcast_convert_type(w & jnp.uint32(0xFFFF0000), F32).astype(BF16)
    right = lax.bitcast_convert_type(w << 16, F32).astype(BF16)
    return jnp.concatenate([left, right], axis=1)


def _outproj_ln_kernel(m_ref, w_ref, x_ref, g_ref, b_ref, o_ref, op_ref):
    mix = jnp.dot(m_ref[...], w_ref[...], preferred_element_type=F32)
    y = _layer_norm(ALPHA * x_ref[...] + mix, g_ref[...], b_ref[...])
    o_ref[...] = y
    op_ref[...] = _pack_bf16_pairs(y)


def _outproj_ln(merged, w_out, x, g, b):
    t = x.shape[0]
    tm = LN_TM
    rowspec = pl.BlockSpec((tm, D_MODEL), lambda i: (i, 0))
    const = lambda shape: pl.BlockSpec(shape, lambda i: (0, 0))
    return pl.pallas_call(
        _outproj_ln_kernel,
        grid=(t // tm,),
        in_specs=[rowspec, const((D_MODEL, D_MODEL)), rowspec, const((1, D_MODEL)), const((1, D_MODEL))],
        out_specs=[rowspec, pl.BlockSpec((tm, D_MODEL // 2), lambda i: (i, 0))],
        out_shape=[jax.ShapeDtypeStruct((t, D_MODEL), F32), jax.ShapeDtypeStruct((t, D_MODEL // 2), jnp.uint32)],
        compiler_params=_cparams("parallel"),
        name="outproj_ln1",
    )(merged, w_out, x, g.reshape(1, -1), b.reshape(1, -1))


def _router_kernel(x_ref, w_ref, b_ref, o_ref):
    lg = jnp.dot(x_ref[...], w_ref[...], precision=lax.Precision.HIGHEST,
                 preferred_element_type=F32) + b_ref[...]
    lane = lax.broadcasted_iota(jnp.int32, lg.shape, 1)
    big = jnp.int32(ROUTE_W)
    neg = -jnp.inf
    g_l = jnp.where(lane < N_EGROUPS, lg, neg)
    g_max = jnp.max(g_l, axis=-1, keepdims=True)
    g_sum = jnp.sum(jnp.exp(g_l - g_max), axis=-1, keepdims=True)
    g_w = 1.0 / g_sum
    g_idx = jnp.min(jnp.where(g_l == g_max, lane, big), axis=-1, keepdims=True)
    e_lo = N_EGROUPS + g_idx * E_PER_GROUP
    e_l = jnp.where(lane >= e_lo, jnp.where(lane < e_lo + E_PER_GROUP, lg, neg), neg)
    e_max = jnp.max(e_l, axis=-1, keepdims=True)
    e_sum = jnp.sum(jnp.exp(e_l - e_max), axis=-1, keepdims=True)
    i1 = jnp.min(jnp.where(e_l == e_max, lane, big), axis=-1, keepdims=True)
    e_l2 = jnp.where(lane == i1, neg, e_l)
    e_max2 = jnp.max(e_l2, axis=-1, keepdims=True)
    i2 = jnp.min(jnp.where(e_l2 == e_max2, lane, big), axis=-1, keepdims=True)
    p1 = 1.0 / e_sum
    p2 = jnp.exp(e_max2 - e_max) / e_sum
    w1 = g_w * (p1 / (p1 + p2))
    w2 = g_w * (p2 / (p1 + p2))
    id1 = (i1 - N_EGROUPS).astype(F32)
    id2 = (i2 - N_EGROUPS).astype(F32)
    out = jnp.where(lane == 0, id1, jnp.where(lane == 1, id2, jnp.where(lane == 2, w1, jnp.where(lane == 3, w2, 0.0))))
    o_ref[...] = out


def _router(x, w_r, b_r):
    t = x.shape[0]
    tm = ROUTE_TM
    return pl.pallas_call(
        _router_kernel,
        grid=(t // tm,),
        in_specs=[pl.BlockSpec((tm, D_MODEL), lambda i: (i, 0)),
                  pl.BlockSpec((D_MODEL, ROUTE_W), lambda i: (0, 0)),
                  pl.BlockSpec((1, ROUTE_W), lambda i: (0, 0))],
        out_specs=pl.BlockSpec((tm, ROUTE_W), lambda i: (i, 0)),
        out_shape=jax.ShapeDtypeStruct((t, ROUTE_W), F32),
        compiler_params=_cparams("parallel"),
        name="router",
    )(x, w_r, b_r)


DMA_UNROLL = 8


def _rows_copy(src, dst, sem, n_rows):
    return pltpu.make_async_copy(src.at[pl.ds(0, n_rows), :], dst.at[pl.ds(0, n_rows), :], sem)


def _wait_rows(src, dst, sem, n):
    n_tiled = pl.multiple_of(n // SUBLANE * SUBLANE, SUBLANE)

    @pl.when(n_tiled > 0)
    def _():
        _rows_copy(src, dst, sem, n_tiled).wait()

    def body(r, carry):
        _rows_copy(src, dst, sem, 1).wait()
        return carry
    lax.fori_loop(0, n - n_tiled, body, 0)


def _for_rows(n, fn):
    def group(i, carry):
        for u in range(DMA_UNROLL):
            fn(i * DMA_UNROLL + u, u % 2)
        return carry

    def single(r, carry):
        fn(r, 0)
        return carry
    n_groups = n // DMA_UNROLL
    lax.fori_loop(0, n_groups, group, 0)
    lax.fori_loop(n_groups * DMA_UNROLL, n, single, 0)


def _expert_kernel(be_ref, tok_ref, dst_ref, nv_ref, x_hbm, wg_ref, wu_ref, wd_ref, y_hbm,
                   xbuf0, xbuf1, obuf0, obuf1, wg_b, wu_b, wd_b, gsem, ssem, *, n_blocks):
    b = pl.program_id(0)
    blk = EXPERT_BLK
    xbufs = (xbuf0, xbuf1)
    obufs = (obuf0, obuf1)

    def gather(block, s):
        def fetch(r, priority):
            tok = tok_ref[block * blk + r]
            pltpu.make_async_copy(x_hbm.at[pl.ds(tok, 1), :], xbufs[s].at[pl.ds(r, 1), :],
                                  gsem.at[s]).start(priority=priority)
        _for_rows(nv_ref[block], fetch)

    def scatter(block, s):
        def send(r, priority):
            row = dst_ref[block * blk + r]
            pltpu.make_async_copy(obufs[s].at[pl.ds(r, 1), :], y_hbm.at[pl.ds(row, 1), :],
                                  ssem.at[s]).start(priority=priority)
        _for_rows(nv_ref[block], send)

    @pl.when(b == 0)
    def _():
        for s in range(2):
            xbufs[s][...] = jnp.zeros_like(xbufs[s])
        gather(0, 0)

    @pl.when((b == 0) | (be_ref[b] != be_ref[jnp.maximum(b - 1, 0)]))
    def _():
        wg_b[...] = wg_ref[...].astype(BF16)
        wu_b[...] = wu_ref[...].astype(BF16)
        wd_b[...] = wd_ref[...].astype(BF16)

    def step(s):
        @pl.when(b + 1 < n_blocks)
        def _():
            gather(jnp.minimum(b + 1, n_blocks - 1), 1 - s)

        _wait_rows(x_hbm, xbufs[s], gsem.at[s], nv_ref[b])

        @pl.when(b >= 2)
        def _():
            _wait_rows(obufs[s], y_hbm, ssem.at[s], nv_ref[jnp.maximum(b - 2, 0)])

        @pl.when(nv_ref[b] > 0)
        def _():
            x = _unpack_bf16_pairs(xbufs[s][...])
            h = _silu(jnp.dot(x, wg_b[...], preferred_element_type=F32)) * \
                jnp.dot(x, wu_b[...], preferred_element_type=F32)
            obufs[s][...] = jnp.dot(h.astype(BF16), wd_b[...], preferred_element_type=F32)
            scatter(b, s)

    for s in range(2):
        pl.when(b % 2 == s)(functools.partial(step, s))

    @pl.when(b == n_blocks - 1)
    def _():
        for back in range(2):
            s = (n_blocks - 1 - back) % 2
            _wait_rows(obufs[s], y_hbm, ssem.at[s], nv_ref[n_blocks - 1 - back])


def _experts(x, w_gate, w_up, w_down, layer, block_expert, slot_tok, slot_dst, n_valid):
    nb = block_expert.shape[0]
    t = x.shape[0]
    assert nb >= 2
    wspec = lambda r, c: pl.BlockSpec((None, None, r, c), lambda b, be, st, sd, nv: (layer, be[b], 0, 0))
    row_buf = pltpu.VMEM((EXPERT_BLK, D_MODEL), F32)
    in_buf = pltpu.VMEM((EXPERT_BLK, D_MODEL // 2), jnp.uint32)
    grid_spec = pltpu.PrefetchScalarGridSpec(
        num_scalar_prefetch=4,
        grid=(nb,),
        in_specs=[
            pl.BlockSpec(memory_space=pl.ANY),
            wspec(D_MODEL, D_EXPERT), wspec(D_MODEL, D_EXPERT), wspec(D_EXPERT, D_MODEL),
        ],
        out_specs=pl.BlockSpec(memory_space=pl.ANY),
        scratch_shapes=[in_buf, in_buf, row_buf, row_buf,
                        pltpu.VMEM((D_MODEL, D_EXPERT), BF16),
                        pltpu.VMEM((D_MODEL, D_EXPERT), BF16),
                        pltpu.VMEM((D_EXPERT, D_MODEL), BF16),
                        pltpu.SemaphoreType.DMA((2,)),
                        pltpu.SemaphoreType.DMA((2,))],
    )
    return pl.pallas_call(
        functools.partial(_expert_kernel, n_blocks=nb),
        grid_spec=grid_spec,
        out_shape=jax.ShapeDtypeStruct((TOP_K * t, D_MODEL), F32),
        compiler_params=_cparams("arbitrary", disable_bounds_checks=True),
        name="experts",
    )(block_expert, slot_tok, slot_dst, n_valid, x, w_gate, w_up, w_down)


def _combine_ln_kernel(y1_ref, y2_ref, route_ref, x_ref, g_ref, b_ref, o_ref, ob_ref):
    route = route_ref[...]
    ffn = y1_ref[...] * route[:, 2:3] + y2_ref[...] * route[:, 3:4]
    y = _layer_norm(ALPHA * x_ref[...] + ffn, g_ref[...], b_ref[...])
    o_ref[...] = y
    ob_ref[...] = y.astype(BF16)


def _combine_ln(y_asg, route, x, g, b):
    t = x.shape[0]
    tm = LN_TM
    rowspec = pl.BlockSpec((tm, D_MODEL), lambda i: (i, 0))
    const = pl.BlockSpec((1, D_MODEL), lambda i: (0, 0))
    return pl.pallas_call(
        _combine_ln_kernel,
        grid=(t // tm,),
        in_specs=[rowspec,
                  pl.BlockSpec((tm, D_MODEL), lambda i: (i + t // tm, 0)),
                  pl.BlockSpec((tm, ROUTE_W), lambda i: (i, 0)),
                  rowspec, const, const],
        out_specs=[rowspec, rowspec],
        out_shape=[jax.ShapeDtypeStruct((t, D_MODEL), F32), jax.ShapeDtypeStruct((t, D_MODEL), BF16)],
        compiler_params=_cparams("parallel"),
        name="combine_ln2",
    )(y_asg, y_asg, route, x, g.reshape(1, -1), b.reshape(1, -1))


def _routing_tables(route, t):
    a_n = t * TOP_K
    e_flat = route[:, 0:TOP_K].astype(jnp.int32).T.reshape(a_n)
    order = jnp.argsort(e_flat).astype(jnp.int32)
    counts = jnp.sum((e_flat[:, None] == jnp.arange(N_EXPERTS, dtype=jnp.int32)[None, :]).astype(jnp.int32), axis=0)
    padded = (counts + EXPERT_BLK - 1) // EXPERT_BLK * EXPERT_BLK
    starts = jnp.cumsum(counts) - counts
    pends = jnp.cumsum(padded)
    pstarts = pends - padded
    n_blocks = a_n // EXPERT_BLK + N_EXPERTS
    block = jnp.arange(n_blocks, dtype=jnp.int32)
    block_start = block * EXPERT_BLK
    block_expert = jnp.minimum(jnp.sum((block_start[:, None] >= pends[None, :]).astype(jnp.int32), axis=1),
                               N_EXPERTS - 1).astype(jnp.int32)
    in_seg = block_start - pstarts[block_expert]
    n_valid = jnp.clip(counts[block_expert] - in_seg, 0, EXPERT_BLK)
    r = jnp.arange(EXPERT_BLK, dtype=jnp.int32)[None, :]
    src = (starts[block_expert] + in_seg)[:, None] + r
    valid = r < n_valid[:, None]
    asg = jnp.where(valid, order[jnp.clip(src, 0, a_n - 1)], 0)
    slot_tok = jnp.where(asg >= t, asg - t, asg).reshape(-1).astype(jnp.int32)
    slot_dst = asg.reshape(-1).astype(jnp.int32)
    return slot_tok, slot_dst, block_expert, n_valid.astype(jnp.int32)


def _seq_flags(groups, tile):
    first, last = [], []
    for batch, length in groups:
        n = length // tile
        for _ in range(batch):
            first += [1] + [0] * (n - 1)
            last += [0] * (n - 1) + [1]
    return np.asarray(first, np.int32), np.asarray(last, np.int32)


def _trunk(x, xb, groups, w_in, conv_w, conv_b, dt_bias, a_log, d_skip, ssd_norm_w, sgu_norm_w, sgu_norm_b,
           sgu_w, sgu_b, w_branch, w_out, ln1_g, ln1_b, ln2_g, ln2_b,
           w_router_group, b_router_group, w_router_expert, b_router_expert, w_gate, w_up, w_down):
    t = x.shape[0]
    depth = w_in.shape[0]
    conv_first, conv_last = _seq_flags(groups, CONV_TM)
    chunk_first, chunk_last = _seq_flags(groups, CHUNK)
    reset_f = jnp.asarray(chunk_first)
    reset_b = jnp.asarray(chunk_last[::-1].copy())
    conv_first, conv_last = jnp.asarray(conv_first), jnp.asarray(conv_last)
    pad_r = ROUTE_W - N_EGROUPS - N_EXPERTS
    for i in range(depth):
        z = _proj(xb, w_in, i, 0, OFF_XBC, MM_TM, MM_TN, "proj_z")
        xbc = _proj(xb, w_in, i, OFF_XBC, CONV_DIM, MM_TM, MM_TN, "proj_xbc")
        dt_raw = _proj(xb, w_in, i, OFF_DT, 2 * N_HEADS, MM_TM, LANE, "proj_dt")
        uv = _proj(xb, w_in, i, OFF_UV, 2 * D_GMLP, MM_TM, MM_TN, "proj_uv")
        gates = _proj(xb, w_in, i, OFF_GATE, 2 * D_MODEL, MM_TM, MM_TN, "proj_gates")

        xbc = _conv_silu(xbc, conv_w[i], conv_b[i], conv_first, conv_last)
        pre = _ssd_pre(dt_raw, dt_bias[i].reshape(1, 2 * N_HEADS), a_log[i].reshape(1, 2 * N_HEADS))
        y_f = _ssd(xbc, pre, reset_f, rev=False)
        dskip = jnp.repeat(d_skip[i], HEAD_DIM).reshape(1, D_INNER)
        y_ssd = _ssd(xbc, pre, reset_b, rev=True,
                     extra=(y_f, z, dskip, ssd_norm_w[i].reshape(1, D_INNER)))

        sgu_b_full = jnp.repeat(sgu_b[i].T, GMLP_GW, axis=1)
        y_gmlp = _gmlp(uv, sgu_norm_w[i], sgu_norm_b[i], sgu_w[i].astype(BF16), sgu_b_full)

        merged = _merge(y_ssd, y_gmlp, w_branch[i].astype(BF16), gates)
        x, x_packed = _outproj_ln(merged, w_out[i].astype(BF16), x, ln1_g[i], ln1_b[i])

        w_r = jnp.concatenate([w_router_group[i], w_router_expert[i],
                               jnp.zeros((D_MODEL, pad_r), F32)], axis=1)
        b_r = jnp.concatenate([b_router_group[i], b_router_expert[i], jnp.zeros((pad_r,), F32)]).reshape(1, ROUTE_W)
        route = _router(x, w_r, b_r)
        slot_tok, slot_dst, block_expert, n_valid = _routing_tables(route, t)
        y_asg = _experts(x_packed, w_gate, w_up, w_down, i, block_expert, slot_tok, slot_dst, n_valid)
        x, xb = _combine_ln(y_asg, route, x, ln2_g[i], ln2_b[i])
    return x


def kernel(x_prompt, x_sample, w_in, conv_w, conv_b, dt_bias, a_log, d_skip, ssd_norm_w, sgu_norm_w, sgu_norm_b, sgu_w, sgu_b, w_branch, w_out, ln1_g, ln1_b, ln2_g, ln2_b, w_router_group, b_router_group, w_router_expert, b_router_expert, w_gate, w_up, w_down):
    groups = (x_prompt.shape[:2], x_sample.shape[:2])
    for _, length in groups:
        assert length % CONV_TM == 0 and length % CHUNK == 0
    x = jnp.concatenate([x_prompt.reshape(-1, D_MODEL), x_sample.reshape(-1, D_MODEL)], axis=0)
    t = x.shape[0]
    assert t % MM_TM == 0 and (t * TOP_K) % EXPERT_BLK == 0 and t % (PRE_CHUNKS * CHUNK) == 0
    y = _trunk(x, x.astype(BF16), groups, w_in, conv_w, conv_b, dt_bias, a_log, d_skip, ssd_norm_w,
               sgu_norm_w, sgu_norm_b, sgu_w, sgu_b, w_branch, w_out, ln1_g, ln1_b, ln2_g, ln2_b,
               w_router_group, b_router_group, w_router_expert, b_router_expert, w_gate, w_up, w_down)
    n_p = x_prompt.shape[0] * x_prompt.shape[1]
    return (y[:n_p].reshape(x_prompt.shape), y[n_p:].reshape(x_sample.shape))
```
